```python
import jax, jax.numpy as jnp
from jax import lax
import numpy as np

D_MODEL = 1024
BATCH = 2
SEQ = 8192
DEPTH = 4
DEC_BATCH = 32
DEC_SEQ = 8
PAST_LEN = 8192
PAGE_SIZE = 128

H_SB = 4
DH_SB = 64
H_HG = 4
DK_HG = 128
DV_HG = 64
HG_CHUNK = 64
H_DF = 4
DH_DF = 64
DV_DF = 2 * DH_DF
W_SB = H_SB * DH_SB
W_HG = H_HG * DV_HG
W_DF = H_DF * DV_DF
MIX_WIDTH = W_SB + W_HG + W_DF
IN_SB = 3 * W_SB
IN_HG = 2 * H_HG * DK_HG + 2 * W_HG
IN_DF = 2 * (H_DF * 2 * DH_DF) + W_DF
IN_WIDTH = IN_SB + IN_HG + IN_DF
PEER_HEADS = 8
PEER_NKEYS = 128
PEER_EXPERTS = PEER_NKEYS * PEER_NKEYS
PEER_DKEY = 256
PEER_TOPK = 16
PEER_BLOCK = 128
Q_BLOCK = 128
EPS = 1e-6
F32 = jnp.float32

kernel_name = "hymba_sb_hgrn2_diff_peer_step"


def _rmsnorm(x, g):
    xf = x.astype(F32)
    y = xf * lax.rsqrt(jnp.mean(xf * xf, axis=-1, keepdims=True) + EPS)
    return (y * g.astype(F32)).astype(x.dtype)


def _head_rmsnorm(o, g):
    return _rmsnorm(o, g.reshape(o.shape[-2], o.shape[-1]))


def _sweep_queries(fn, q, q_pos):
    B, T = q.shape[0], q.shape[1]
    if T <= Q_BLOCK or T % Q_BLOCK:
        return fn(q, q_pos)
    nb = T // Q_BLOCK
    qb = jnp.moveaxis(q.reshape(B, nb, Q_BLOCK, *q.shape[2:]), 1, 0)
    pb = q_pos.reshape(nb, Q_BLOCK)
    out = lax.map(lambda a: fn(a[0], a[1]), (qb, pb))
    out = jnp.moveaxis(out, 0, 1)
    return out.reshape(B, T, *out.shape[3:])


def _stick_breaking(q, k, v, q_pos, k_pos):
    scale = DH_SB ** -0.5
    kf, vf = k.astype(F32), v.astype(F32)

    def block(qb, pb):
        z = jnp.einsum('bthd,bshd->bhts', qb.astype(F32), kf) * scale
        mask = k_pos[None, :] < pb[:, None]
        log_stay = jnp.where(mask, jax.nn.log_sigmoid(-z), 0.0)
        log_surv = lax.cumsum(log_stay, axis=3, reverse=True) - log_stay
        w = jnp.where(mask, jnp.exp(jax.nn.log_sigmoid(z) + log_surv), 0.0)
        return jnp.einsum('bhts,bshd->bthd', w, vf)

    return _sweep_queries(block, q, q_pos)


def _diff_attention(q, k, v, lam, q_pos, k_pos):
    scale = DH_DF ** -0.5
    kf, vf = k.astype(F32), v.astype(F32)

    def block(qb, pb):
        s = jnp.einsum('bthmd,bshmd->bhmts', qb.astype(F32), kf) * scale
        mask = k_pos[None, :] <= pb[:, None]
        p = jax.nn.softmax(jnp.where(mask, s, -jnp.inf), axis=-1)
        a = p[:, :, 0] - lam * p[:, :, 1]
        return jnp.einsum('bhts,bshe->bthe', a, vf)

    return _sweep_queries(block, q, q_pos)


def _hgrn2(q, log_f, k, i, s0):
    B, T, H, DK = q.shape
    L = min(HG_CHUNK, T)
    pad = (-T) % L
    nc = (T + pad) // L

    def prep(a):
        a = jnp.pad(a.astype(F32), ((0, 0), (0, pad), (0, 0), (0, 0)))
        return jnp.moveaxis(a.reshape(B, nc, L, *a.shape[2:]), 1, 0)

    tri = jnp.tril(jnp.ones((L, L), dtype=bool))[None, :, :, None, None]

    def step(S, inp):
        qc, lfc, kc, ic = inp
        b = jnp.cumsum(lfc, axis=1)
        o_inter = jnp.einsum('blhk,bhkv->blhv', qc * jnp.exp(b), S)
        rel = b[:, :, None] - b[:, None, :]
        decay = jnp.exp(jnp.where(tri, rel, -jnp.inf))
        att = jnp.einsum('bthk,bshk,btshk->bhts', qc, kc, decay)
        o_intra = jnp.einsum('bhts,bshv->bthv', att, ic)
        bL = b[:, -1]
        S = jnp.exp(bL)[..., None] * S + jnp.einsum('bshk,bshv->bhkv', kc * jnp.exp(bL[:, None] - b), ic)
        return S, o_inter + o_intra

    S, o = lax.scan(step, s0.astype(F32), (prep(q), prep(log_f), prep(k), prep(i)))
    o = jnp.moveaxis(o, 0, 1).reshape(B, nc * L, H, i.shape[-1])[:, :T]
    return o, S


def _peer(h, w_pq, k1, k2, u, v):
    B, T, D = h.shape
    n = B * T
    blk = PEER_BLOCK if n % PEER_BLOCK == 0 else n
    hb = h.reshape(n // blk, blk, D)
    k1f, k2f = k1.astype(F32), k2.astype(F32)

    def block(xb):
        q = (xb @ w_pq).astype(F32).reshape(blk, PEER_HEADS, 2, PEER_DKEY // 2)
        s1 = jnp.einsum('nhd,hkd->nhk', q[:, :, 0], k1f)
        s2 = jnp.einsum('nhd,hkd->nhk', q[:, :, 1], k2f)
        v1, i1 = lax.top_k(s1, PEER_TOPK)
        v2, i2 = lax.top_k(s2, PEER_TOPK)
        cand_s = (v1[..., :, None] + v2[..., None, :]).reshape(blk, PEER_HEADS, PEER_TOPK * PEER_TOPK)
        cand_e = (i1[..., :, None] * PEER_NKEYS + i2[..., None, :]).reshape(blk, PEER_HEADS, PEER_TOPK * PEER_TOPK)
        top_s, top_j = lax.top_k(cand_s, PEER_TOPK)
        eid = jnp.take_along_axis(cand_e, top_j, axis=-1)
        gate = jax.nn.softmax(top_s, axis=-1)
        act = jax.nn.gelu(jnp.einsum('nhed,nd->nhe', u[eid].astype(F32), xb.astype(F32)), approximate=False)
        return jnp.einsum('nhe,nhed->nd', gate * act, v[eid].astype(F32))

    out = lax.map(block, hb)
    return out.reshape(B, T, D).astype(h.dtype)


def _layer(x, c, past_sb_k, past_sb_v, past_df_k, past_df_v, s0, lb, lam_init,
           w_ada, b_ada, norm1_g, norm2_g, w_in, lam_q1, lam_k1, lam_q2, lam_k2,
           gn_sb, gn_hg, gn_df, w_out, w_pq, peer_k1, peer_k2, peer_u, peer_v):
    B, T, _ = x.shape
    n_past = past_sb_k.shape[1]
    q_pos = n_past + jnp.arange(T, dtype=jnp.int32)
    k_pos = jnp.arange(n_past + T, dtype=jnp.int32)
    mod = (jax.nn.silu(c) @ w_ada + b_ada)[:, None, :]
    sh1, sc1, g1, sh2, sc2, g2 = jnp.split(mod, 6, axis=-1)

    h = _rmsnorm(x, norm1_g) * (1 + sc1) + sh1
    z = h @ w_in
    z_sb = z[..., :IN_SB]
    z_hg = z[..., IN_SB:IN_SB + IN_HG]
    z_df = z[..., IN_SB + IN_HG:]

    q_sb = z_sb[..., :W_SB].reshape(B, T, H_SB, DH_SB)
    k_sb = z_sb[..., W_SB:2 * W_SB].reshape(B, T, H_SB, DH_SB)
    v_sb = z_sb[..., 2 * W_SB:].reshape(B, T, H_SB, DH_SB)
    o_sb = _stick_breaking(q_sb, jnp.concatenate([past_sb_k, k_sb], axis=1),
                           jnp.concatenate([past_sb_v, v_sb], axis=1), q_pos, k_pos)

    nk = H_HG * DK_HG
    q_hg = z_hg[..., :nk].reshape(B, T, H_HG, DK_HG)
    a_f = z_hg[..., nk:2 * nk].reshape(B, T, H_HG, DK_HG).astype(F32)
    i_hg = z_hg[..., 2 * nk:2 * nk + W_HG].reshape(B, T, H_HG, DV_HG)
    gate = z_hg[..., 2 * nk + W_HG:].astype(F32)
    lb = lb.reshape(H_HG, DK_HG)
    log_f = jnp.logaddexp(jnp.log(lb), jnp.log1p(-lb) + jax.nn.log_sigmoid(a_f))
    k_hg = (1.0 - lb) * jax.nn.sigmoid(-a_f)
    o_hg, s_new = _hgrn2(q_hg, log_f, k_hg, i_hg, s0)

    nqk = H_DF * 2 * DH_DF
    q_df = z_df[..., :nqk].reshape(B, T, H_DF, 2, DH_DF)
    k_df = z_df[..., nqk:2 * nqk].reshape(B, T, H_DF, 2 * DH_DF)
    v_df = z_df[..., 2 * nqk:].reshape(B, T, H_DF, DV_DF)
    lam = (jnp.exp(jnp.sum(lam_q1.astype(F32) * lam_k1.astype(F32)))
           - jnp.exp(jnp.sum(lam_q2.astype(F32) * lam_k2.astype(F32))) + lam_init)
    k_all = jnp.concatenate([past_df_k, k_df], axis=1).reshape(B, n_past + T, H_DF, 2, DH_DF)
    v_all = jnp.concatenate([past_df_v, v_df], axis=1)
    o_df = _diff_attention(q_df, k_all, v_all, lam, q_pos, k_pos)

    o = jnp.concatenate([
        _head_rmsnorm(o_sb, gn_sb).reshape(B, T, W_SB),
        _head_rmsnorm(o_hg, gn_hg).reshape(B, T, W_HG) * jax.nn.silu(gate),
        (_head_rmsnorm(o_df, gn_df) * (1.0 - lam_init)).reshape(B, T, W_DF)], axis=-1).astype(x.dtype)
    x = x + g1 * (o @ w_out)

    h2 = _rmsnorm(x, norm2_g) * (1 + sc2) + sh2
    x = x + g2 * _peer(h2, w_pq, peer_k1, peer_k2, peer_u, peer_v)
    return x, k_sb, v_sb, k_df, v_df, s_new.astype(x.dtype)


def setup_inputs(seed: int = 0) -> dict:
    key = jax.random.key(seed)
    ks = iter(jax.random.split(key, 40))

    def nrm(shape, s):
        return jax.random.normal(next(ks), shape, F32) * s

    n_pages = PAST_LEN // PAGE_SIZE
    n_used = DEC_BATCH * n_pages
    n_pool = n_used + n_used // 4
    perm = jax.random.permutation(next(ks), n_pool)
    page_table = perm[:n_used].reshape(DEC_BATCH, n_pages).astype(jnp.int32)
    dinv = D_MODEL ** -0.5
    return {
        "x_prompt": nrm((BATCH, SEQ, D_MODEL), 1.0),
        "x_sample": nrm((DEC_BATCH, DEC_SEQ, D_MODEL), 1.0),
        "cache_sb_k": nrm((DEPTH, n_pool, PAGE_SIZE, H_SB, DH_SB), 1.0),
        "cache_sb_v": nrm((DEPTH, n_pool, PAGE_SIZE, H_SB, DH_SB), 1.0),
        "cache_df_k": nrm((DEPTH, n_pool, PAGE_SIZE, H_DF, 2 * DH_DF), 1.0),
        "cache_df_v": nrm((DEPTH, n_pool, PAGE_SIZE, H_DF, DV_DF), 1.0),
        "state_hgrn": nrm((DEPTH, DEC_BATCH, H_HG, DK_HG, DV_HG), 0.5),
        "page_table": page_table,
        "c_prompt": nrm((BATCH, D_MODEL), 1.0),
        "c_sample": nrm((DEC_BATCH, D_MODEL), 1.0),
        "w_ada": nrm((DEPTH, D_MODEL, 6 * D_MODEL), 0.3 * dinv),
        "b_ada": nrm((DEPTH, 6 * D_MODEL), 0.02),
        "norm1_g": 1.0 + nrm((DEPTH, D_MODEL), 0.02),
        "norm2_g": 1.0 + nrm((DEPTH, D_MODEL), 0.02),
        "w_in": nrm((DEPTH, D_MODEL, IN_WIDTH), dinv),
        "lb_logits": nrm((DEPTH, H_HG * DK_HG), 0.5),
        "lam_q1": nrm((DEPTH, DH_DF), 0.1),
        "lam_k1": nrm((DEPTH, DH_DF), 0.1),
        "lam_q2": nrm((DEPTH, DH_DF), 0.1),
        "lam_k2": nrm((DEPTH, DH_DF), 0.1),
        "gn_sb": 1.0 + nrm((DEPTH, W_SB), 0.02),
        "gn_hg": 1.0 + nrm((DEPTH, W_HG), 0.02),
        "gn_df": 1.0 + nrm((DEPTH, W_DF), 0.02),
        "w_out": nrm((DEPTH, MIX_WIDTH, D_MODEL), MIX_WIDTH ** -0.5),
        "w_pq": nrm((DEPTH, D_MODEL, PEER_HEADS * PEER_DKEY), dinv),
        "peer_k1": nrm((DEPTH, PEER_HEADS, PEER_NKEYS, PEER_DKEY // 2), (PEER_DKEY // 2) ** -0.5),
        "peer_k2": nrm((DEPTH, PEER_HEADS, PEER_NKEYS, PEER_DKEY // 2), (PEER_DKEY // 2) ** -0.5),
        "peer_u": nrm((DEPTH, PEER_EXPERTS, D_MODEL), dinv),
        "peer_v": nrm((DEPTH, PEER_EXPERTS, D_MODEL), dinv),
        "final_g": 1.0 + nrm((D_MODEL,), 0.02),
    }


def reference(x_prompt, x_sample, cache_sb_k, cache_sb_v, cache_df_k, cache_df_v, state_hgrn,
              page_table, c_prompt, c_sample, w_ada, b_ada, norm1_g, norm2_g, w_in, lb_logits,
              lam_q1, lam_k1, lam_q2, lam_k2, gn_sb, gn_hg, gn_df, w_out, w_pq,
              peer_k1, peer_k2, peer_u, peer_v, final_g):
    lb_cum = jnp.cumsum(jax.nn.softmax(lb_logits.astype(F32), axis=0), axis=0)
    lb_all = lb_cum - lb_cum[0]

    def gather(cache_l):
        g = cache_l[page_table]
        return g.reshape(g.shape[0], g.shape[1] * g.shape[2], *g.shape[3:])

    bp = x_prompt.shape[0]
    e_sb = jnp.zeros((bp, 0, H_SB, DH_SB), x_prompt.dtype)
    e_dk = jnp.zeros((bp, 0, H_DF, 2 * DH_DF), x_prompt.dtype)
    e_dv = jnp.zeros((bp, 0, H_DF, DV_DF), x_prompt.dtype)
    s0_p = jnp.zeros((bp, H_HG, DK_HG, DV_HG), F32)

    yp, ys = x_prompt, x_sample
    p_sbk, p_sbv, p_dk, p_dv, p_s = [], [], [], [], []
    s_sbk, s_sbv, s_dk, s_dv, s_s = [], [], [], [], []
    for l in range(DEPTH):
        lam_init = 0.8 - 0.6 * float(np.exp(-0.3 * l))
        lw = (w_ada[l], b_ada[l], norm1_g[l], norm2_g[l], w_in[l], lam_q1[l], lam_k1[l],
              lam_q2[l], lam_k2[l], gn_sb[l], gn_hg[l], gn_df[l], w_out[l], w_pq[l],
              peer_k1[l], peer_k2[l], peer_u[l], peer_v[l])
        yp, a, b, cc, d, e = _layer(yp, c_prompt, e_sb, e_sb, e_dk, e_dv, s0_p, lb_all[l], lam_init, *lw)
        p_sbk.append(a); p_sbv.append(b); p_dk.append(cc); p_dv.append(d); p_s.append(e)
        ys, a, b, cc, d, e = _layer(ys, c_sample, gather(cache_sb_k[l]), gather(cache_sb_v[l]),
                                    gather(cache_df_k[l]), gather(cache_df_v[l]), state_hgrn[l],
                                    lb_all[l], lam_init, *lw)
        s_sbk.append(a); s_sbv.append(b); s_dk.append(cc); s_dv.append(d); s_s.append(e)

    y_prompt = _rmsnorm(yp, final_g)
    y_sample = _rmsnorm(ys, final_g)
    return (y_prompt, y_sample,
            jnp.stack(p_sbk), jnp.stack(p_sbv), jnp.stack(p_dk), jnp.stack(p_dv), jnp.stack(p_s),
            jnp.stack(s_sbk), jnp.stack(s_sbv), jnp.stack(s_dk), jnp.stack(s_dv), jnp.stack(s_s))
```

```python
import functools
import math

import numpy as np
import jax
import jax.numpy as jnp
from jax import lax
from jax.experimental import pallas as pl
from jax.experimental.pallas import tpu as pltpu

F32 = jnp.float32
BF16 = jnp.bfloat16
EPS = 1e-6
LANES = 128
VMEM_LIMIT = 48 * 1024 * 1024
SB_DEAD = -120.0

H_SB, DH_SB = 4, 64
H_HG, DK_HG, DV_HG = 4, 128, 64
H_DF, DH_DF, DV_DF = 4, 64, 128
W_SB, W_HG, W_DF = H_SB * DH_SB, H_HG * DV_HG, H_DF * DV_DF
NK_HG = H_HG * DK_HG
NQK_DF = H_DF * 2 * DH_DF
PEER_HEADS, PEER_NKEYS, PEER_TOPK = 8, 128, 16
PAGE = 128


def _cparams(sem):
    return pltpu.CompilerParams(dimension_semantics=sem, vmem_limit_bytes=VMEM_LIMIT)


def _mm(a, b):
    return lax.dot_general(a, b, (((1,), (0,)), ((), ())), preferred_element_type=F32)


def _mm_nt(a, b):
    return lax.dot_general(a, b, (((1,), (1,)), ((), ())), preferred_element_type=F32)


def _mm_tn(a, b):
    return lax.dot_general(a, b, (((0,), (0,)), ((), ())), preferred_element_type=F32)


def _split(x):
    hi = x.astype(BF16)
    lo = (x - hi.astype(F32)).astype(BF16)
    return hi, lo


def _mm_f32_lhs(x, m_bf16):
    hi, lo = _split(x)
    return _mm(hi, m_bf16) + _mm(lo, m_bf16)


def _mm_f32_rhs(m_bf16, x):
    hi, lo = _split(x)
    return _mm(m_bf16, hi) + _mm(m_bf16, lo)


def _log_sigmoid_pair(z):
    l1p = jnp.log1p(jnp.exp(-jnp.abs(z)))
    return jnp.minimum(z, 0.0) - l1p, -(jnp.maximum(z, 0.0) + l1p)


def _mod_kernel(c_ref, w_ref, b_ref, o_ref):
    c = c_ref[...]
    a = c * (1.0 / (1.0 + jnp.exp(-c)))
    ahi, alo = _split(a)
    w = w_ref[...]
    whi, wlo = _split(w)
    o_ref[...] = _mm(ahi, whi) + _mm(ahi, wlo) + _mm(alo, whi) + b_ref[...]


def _mod_call(c_all, w_ada, b_ada):
    depth, d, d6 = w_ada.shape
    r = c_all.shape[0]
    tn = 1024
    return pl.pallas_call(
        _mod_kernel,
        grid=(depth, d6 // tn),
        in_specs=[pl.BlockSpec((r, d), lambda l, j: (0, 0)),
                  pl.BlockSpec((None, d, tn), lambda l, j: (l, 0, j)),
                  pl.BlockSpec((None, 1, tn), lambda l, j: (l, 0, j))],
        out_specs=pl.BlockSpec((None, r, tn), lambda l, j: (l, 0, j)),
        out_shape=jax.ShapeDtypeStruct((depth, r, d6), F32),
        compiler_params=_cparams(("parallel", "parallel")),
        name="adaln_mod",
    )(c_all, w_ada, b_ada.reshape(depth, 1, d6))


def _prep_kernel(lb_ref, q1_ref, k1_ref, q2_ref, k2_ref, lbo_ref, lam_ref):
    lg = lb_ref[...]
    depth = lg.shape[0]
    mx = jnp.max(lg, axis=0, keepdims=True)
    e = jnp.exp(lg - mx)
    p = e / jnp.sum(e, axis=0, keepdims=True)
    run = p[0:1]
    rows = [jnp.zeros_like(run)]
    for l in range(1, depth):
        run = run + p[l:l + 1]
        rows.append(run - p[0:1])
    lbo_ref[...] = jnp.concatenate(rows, axis=0)
    s1 = jnp.sum(q1_ref[...] * k1_ref[...], axis=1, keepdims=True)
    s2 = jnp.sum(q2_ref[...] * k2_ref[...], axis=1, keepdims=True)
    lam_ref[...] = jnp.broadcast_to(jnp.exp(s1) - jnp.exp(s2), lam_ref.shape)


def _prep_call(lb_logits, lq1, lk1, lq2, lk2):
    depth, n = lb_logits.shape
    return pl.pallas_call(
        _prep_kernel,
        out_shape=(jax.ShapeDtypeStruct((depth, n), F32),
                   jax.ShapeDtypeStruct((depth, LANES), F32)),
        name="layer_scalars",
    )(lb_logits, lq1, lk1, lq2, lk2)


_IN_SEGS = (
    ("qsb", W_SB, BF16, DH_SB ** -0.5), ("ksb", W_SB, F32, None), ("vsb", W_SB, F32, None),
    ("qhg", NK_HG, F32, None), ("ahg", NK_HG, F32, None), ("ihg", W_HG, BF16, None), ("ghg", W_HG, F32, None),
    ("qdf", NQK_DF, BF16, DH_DF ** -0.5), ("kdf", NQK_DF, F32, None), ("vdf", W_DF, F32, None),
)
_BF16_COPIES = ("ksb", "vsb", "kdf", "vdf")


def _inproj_kernel(x_ref, g_ref, sc_ref, sh_ref, w_ref, *out_refs):
    x = x_ref[...]
    y = x * lax.rsqrt(jnp.mean(x * x, axis=-1, keepdims=True) + EPS) * g_ref[...]
    h = (y * (1.0 + sc_ref[...]) + sh_ref[...]).astype(BF16)
    outs = dict(zip([s[0] for s in _IN_SEGS] + [n + "_b" for n in _BF16_COPIES], out_refs))
    off = 0
    for name, width, dt, scale in _IN_SEGS:
        z = _mm(h, w_ref[:, off:off + width])
        off += width
        outs[name][...] = (z if scale is None else z * scale).astype(dt)
        if name in _BF16_COPIES:
            outs[name + "_b"][...] = z.astype(BF16)


def _inproj_call(x, g, mod, w_in_b, tm):
    G, T, D = x.shape
    R = mod.shape[1]
    rb = 1 if R == 1 else tm
    nt = T // tm
    width = w_in_b.shape[1]
    mrow = (lambda comp: (lambda b, i: (b, 0 if R == 1 else i, comp)))
    names = [s[0] for s in _IN_SEGS] + [n + "_b" for n in _BF16_COPIES]
    widths = {s[0]: s[1] for s in _IN_SEGS}
    dts = {s[0]: s[2] for s in _IN_SEGS}
    for n in _BF16_COPIES:
        widths[n + "_b"], dts[n + "_b"] = widths[n], BF16
    outs = pl.pallas_call(
        _inproj_kernel,
        grid=(G, nt),
        in_specs=[pl.BlockSpec((None, tm, D), lambda b, i: (b, i, 0)),
                  pl.BlockSpec((1, D), lambda b, i: (0, 0)),
                  pl.BlockSpec((None, rb, D), mrow(1)),
                  pl.BlockSpec((None, rb, D), mrow(0)),
                  pl.BlockSpec((D, width), lambda b, i: (0, 0))],
        out_specs=[pl.BlockSpec((None, tm, widths[n]), lambda b, i: (b, i, 0)) for n in names],
        out_shape=[jax.ShapeDtypeStruct((G, T, widths[n]), dts[n]) for n in names],
        compiler_params=_cparams(("parallel", "parallel")),
        name="norm_inproj",
    )(x, g.reshape(1, D), mod, mod, w_in_b)
    return dict(zip(names, outs))


def _sb_block(q, k, v, c, acc, u, mask):
    z = _mm_nt(q, k)
    ls_pos, ls_neg = _log_sigmoid_pair(z)
    if mask is not None:
        ls_neg = jnp.where(mask, ls_neg, 0.0)
    surv = _mm_f32_lhs(ls_neg, u) + c
    w = jnp.exp(ls_pos + surv)
    if mask is not None:
        w = jnp.where(mask, w, 0.0)
    acc = acc + _mm(w.astype(BF16), v)
    c = c + jnp.sum(ls_neg, axis=1, keepdims=True)
    return c, acc


def _sb_prompt_kernel(q_ref, k_ref, v_ref, u_ref, o_ref, *, tq):
    qi = pl.program_id(1)
    u = u_ref[...]
    row = lax.broadcasted_iota(jnp.int32, (tq, tq), 0)
    col = lax.broadcasted_iota(jnp.int32, (tq, tq), 1)
    diag_mask = col < row
    for h in range(H_SB):
        hs = slice(h * DH_SB, (h + 1) * DH_SB)
        q = q_ref[:, hs]
        start = pl.multiple_of(qi * tq, tq)
        c, acc = _sb_block(q, k_ref[pl.ds(start, tq), hs], v_ref[pl.ds(start, tq), hs],
                           jnp.zeros((tq, 1), F32), jnp.zeros((tq, DH_SB), F32), u, diag_mask)

        def cond(carry):
            kb, c, _ = carry
            return jnp.logical_and(kb >= 0, jnp.max(c) > SB_DEAD)

        def body(carry):
            kb, c, acc = carry
            st = pl.multiple_of(kb * tq, tq)
            c, acc = _sb_block(q, k_ref[pl.ds(st, tq), hs], v_ref[pl.ds(st, tq), hs], c, acc, u, None)
            return kb - 1, c, acc

        _, _, acc = lax.while_loop(cond, body, (qi - 1, c, acc))
        o_ref[:, hs] = acc


def _suffix_matrix(n):
    return jnp.asarray(np.tril(np.ones((n, n), np.float32), -1), BF16)


def _sb_prompt_call(q, k, v, tq):
    B, T, W = q.shape
    return pl.pallas_call(
        functools.partial(_sb_prompt_kernel, tq=tq),
        grid=(B, T // tq),
        in_specs=[pl.BlockSpec((None, tq, W), lambda b, i: (b, i, 0)),
                  pl.BlockSpec((None, T, W), lambda b, i: (b, 0, 0)),
                  pl.BlockSpec((None, T, W), lambda b, i: (b, 0, 0)),
                  pl.BlockSpec((tq, tq), lambda b, i: (0, 0))],
        out_specs=pl.BlockSpec((None, tq, W), lambda b, i: (b, i, 0)),
        out_shape=jax.ShapeDtypeStruct((B, T, W), F32),
        compiler_params=_cparams(("parallel", "parallel")),
        name="sb_prompt",
    )(q, k, v, _suffix_matrix(tq))


def _softmax_step(s, v, m, l, acc):
    m_new = jnp.maximum(m, jnp.max(s, axis=1, keepdims=True))
    alpha = jnp.exp(m - m_new)
    p = jnp.exp(s - m_new)
    l = alpha * l + jnp.sum(p, axis=1, keepdims=True)
    acc = alpha * acc + _mm(p.astype(BF16), v)
    return m_new, l, acc


def _df_prompt_kernel(lam_ref, q_ref, k_ref, v_ref, o_ref, *, tq):
    qi = pl.program_id(2)
    q1 = q_ref[:, :DH_DF]
    q2 = q_ref[:, DH_DF:]

    def block(st, carry, mask):
        m1, l1, a1, m2, l2, a2 = carry
        k = k_ref[pl.ds(st, tq), :]
        v = v_ref[pl.ds(st, tq), :]
        s1 = _mm_nt(q1, k[:, :DH_DF])
        s2 = _mm_nt(q2, k[:, DH_DF:])
        if mask is not None:
            s1 = jnp.where(mask, s1, -jnp.inf)
            s2 = jnp.where(mask, s2, -jnp.inf)
        return _softmax_step(s1, v, m1, l1, a1) + _softmax_step(s2, v, m2, l2, a2)

    neg = jnp.full((tq, 1), -jnp.inf, F32)
    zero1 = jnp.zeros((tq, 1), F32)
    zacc = jnp.zeros((tq, DV_DF), F32)
    row = lax.broadcasted_iota(jnp.int32, (tq, tq), 0)
    col = lax.broadcasted_iota(jnp.int32, (tq, tq), 1)
    carry = block(pl.multiple_of(qi * tq, tq), (neg, zero1, zacc, neg, zero1, zacc), col <= row)
    carry = lax.fori_loop(0, qi, lambda kb, c: block(pl.multiple_of(kb * tq, tq), c, None), carry)
    m1, l1, a1, m2, l2, a2 = carry
    o_ref[...] = a1 / l1 - lam_ref[0, 0] * (a2 / l2)


def _df_prompt_call(lam, q, k, v, tq):
    B, T, _ = q.shape
    blk = 2 * DH_DF
    return pl.pallas_call(
        functools.partial(_df_prompt_kernel, tq=tq),
        grid=(B, H_DF, T // tq),
        in_specs=[pl.BlockSpec(memory_space=pltpu.SMEM),
                  pl.BlockSpec((None, tq, blk), lambda b, h, i: (b, i, h)),
                  pl.BlockSpec((None, T, blk), lambda b, h, i: (b, 0, h)),
                  pl.BlockSpec((None, T, DV_DF), lambda b, h, i: (b, 0, h))],
        out_specs=pl.BlockSpec((None, tq, DV_DF), lambda b, h, i: (b, i, h)),
        out_shape=jax.ShapeDtypeStruct((B, T, W_DF), F32),
        compiler_params=_cparams(("parallel", "parallel", "parallel")),
        name="df_prompt",
    )(lam, q, k, v)


def _hgrn_consts(C):
    levels = int(math.log2(C))
    t = np.arange(C)
    tril = (t[:, None] >= t[None, :]).astype(np.float32)
    lmat, rmat, masks = [], [], []
    for lv in range(levels):
        w = C >> (lv + 1)
        blk = t // (2 * w)
        second = (t % (2 * w)) >= w
        mid = blk * 2 * w + w - 1
        j = t[None, :]
        lmat.append(((j > mid[:, None]) & (j <= t[:, None]) & second[:, None]).astype(np.float32))
        rmat.append(((j > t[:, None]) & (j <= mid[:, None]) & (~second)[:, None]).astype(np.float32))
        masks.append(((blk[:, None] == blk[None, :]) & second[:, None] & (~second)[None, :]).astype(np.float32))
    masks.append(np.eye(C, dtype=np.float32))
    lr = np.concatenate([tril] + lmat + rmat, axis=0)
    return jnp.asarray(lr, BF16), jnp.asarray(np.stack(masks), F32), levels


def _hgrn_kernel(q_ref, a_ref, i_ref, lb_ref, s0_ref, lr_ref, mk_ref, o_ref, sT_ref, st_scr, *, C, levels):
    ci = pl.program_id(1)

    @pl.when(ci == 0)
    def _():
        st_scr[...] = s0_ref[...]

    a = a_ref[...]
    lb = lb_ref[...]
    lsa, _ = _log_sigmoid_pair(a)
    x0 = jnp.log(lb)
    y0 = jnp.log1p(-lb) + lsa
    log_f = jnp.maximum(x0, y0) + jnp.log1p(jnp.exp(-jnp.abs(x0 - y0)))
    kk = (1.0 - lb) / (1.0 + jnp.exp(a))
    sums = _mm_f32_rhs(lr_ref[...], log_f)
    b = sums[:C]
    q = q_ref[...]
    q_in = (q * jnp.exp(b)).astype(BF16)
    k_out = (kk * jnp.exp(b[C - 1:C] - b)).astype(BF16)
    qb = q.astype(BF16)
    kb = kk.astype(BF16)
    dec_l = jnp.exp(b[C - 1:C])
    for h in range(H_HG):
        ks = slice(h * DK_HG, (h + 1) * DK_HG)
        vs = slice(h * DV_HG, (h + 1) * DV_HG)
        att = mk_ref[levels] * _mm_nt(qb[:, ks], kb[:, ks])
        for lv in range(levels):
            dq = sums[(1 + lv) * C:(2 + lv) * C, ks]
            dk = sums[(1 + levels + lv) * C:(2 + levels + lv) * C, ks]
            ql = (q[:, ks] * jnp.exp(dq)).astype(BF16)
            kl = (kk[:, ks] * jnp.exp(dk)).astype(BF16)
            att = att + mk_ref[lv] * _mm_nt(ql, kl)
        iv = i_ref[:, vs]
        sT = st_scr[h]
        o_ref[:, vs] = _mm_nt(q_in[:, ks], sT.astype(BF16)) + _mm(att.astype(BF16), iv)
        st_scr[h] = sT * dec_l[:, ks] + _mm_tn(iv, k_out[:, ks])

    @pl.when(ci == pl.num_programs(1) - 1)
    def _():
        sT_ref[...] = st_scr[...]


def _hgrn_call(q, a, iv, lb_row, s0T, C):
    B, T, _ = q.shape
    lr, masks, levels = _hgrn_consts(C)
    nlr = lr.shape[0]
    return pl.pallas_call(
        functools.partial(_hgrn_kernel, C=C, levels=levels),
        grid=(B, T // C),
        in_specs=[pl.BlockSpec((None, C, NK_HG), lambda b, i: (b, i, 0)),
                  pl.BlockSpec((None, C, NK_HG), lambda b, i: (b, i, 0)),
                  pl.BlockSpec((None, C, W_HG), lambda b, i: (b, i, 0)),
                  pl.BlockSpec((1, NK_HG), lambda b, i: (0, 0)),
                  pl.BlockSpec((None, H_HG, DV_HG, DK_HG), lambda b, i: (b, 0, 0, 0)),
                  pl.BlockSpec((nlr, C), lambda b, i: (0, 0)),
                  pl.BlockSpec((levels + 1, C, C), lambda b, i: (0, 0, 0))],
        out_specs=[pl.BlockSpec((None, C, W_HG), lambda b, i: (b, i, 0)),
                   pl.BlockSpec((None, H_HG, DV_HG, DK_HG), lambda b, i: (b, 0, 0, 0))],
        out_shape=[jax.ShapeDtypeStruct((B, T, W_HG), F32),
                   jax.ShapeDtypeStruct((B, H_HG, DV_HG, DK_HG), F32)],
        scratch_shapes=[pltpu.VMEM((H_HG, DV_HG, DK_HG), F32)],
        compiler_params=_cparams(("parallel", "arbitrary")),
        name="hgrn2",
    )(q, a, iv, lb_row, s0T, lr, masks)


def _page_copies(cache_k, cache_v, kbuf, vbuf, semk, semv, layer, pages, slot):
    cps = []
    for j, page in enumerate(pages):
        cps.append(pltpu.make_async_copy(cache_k.at[layer, page], kbuf.at[slot, j], semk.at[slot]))
        cps.append(pltpu.make_async_copy(cache_v.at[layer, page], vbuf.at[slot, j], semv.at[slot]))
    return cps


def _sb_sample_kernel(pt_ref, q_ref, kn_ref, vn_ref, u_ref, un_ref, ck_ref, cv_ref, o_ref,
                      kbuf, vbuf, semk, semv, *, layer, P, n_tok):
    b = pl.program_id(0)
    nb = pl.num_programs(0)
    n_pages = pt_ref.shape[1]
    nch = n_pages // P
    slot = b % 2

    def copies(bb, ch, sl):
        pages = [pt_ref[bb, n_pages - 1 - ch * P - j] for j in range(P)]
        return _page_copies(ck_ref, cv_ref, kbuf, vbuf, semk, semv, layer, pages, sl)

    @pl.when(b == 0)
    def _():
        for cp in copies(0, 0, 0):
            cp.start()

    @pl.when(b + 1 < nb)
    def _():
        for cp in copies(b + 1, 0, 1 - slot):
            cp.start()

    q = q_ref[...]
    rows = q.shape[0]
    nn = kn_ref.shape[0]
    t_of_row = lax.broadcasted_iota(jnp.int32, (rows, nn), 0) % n_tok
    j_new = lax.broadcasted_iota(jnp.int32, (rows, nn), 1)
    mask_new = jnp.logical_and(j_new < t_of_row, j_new < n_tok)
    c, acc = _sb_block(q, kn_ref[...], vn_ref[...], jnp.zeros((rows, 1), F32),
                       jnp.zeros((rows, W_SB), F32), un_ref[...], mask_new)
    u = u_ref[...]

    def chunk(sl, c, acc):
        for j in range(P):
            c, acc = _sb_block(q, kbuf[sl, j].astype(BF16), vbuf[sl, j].astype(BF16), c, acc, u, None)
        return c, acc

    for cp in copies(b, 0, slot):
        cp.wait()
    c, acc = chunk(slot, c, acc)

    def cond(carry):
        ch, c, _ = carry
        return jnp.logical_and(ch < nch, jnp.max(c) > SB_DEAD)

    def body(carry):
        ch, c, acc = carry
        cps = copies(b, ch, 2)
        for cp in cps:
            cp.start()
        for cp in cps:
            cp.wait()
        c, acc = chunk(2, c, acc)
        return ch + 1, c, acc

    _, _, acc = lax.while_loop(cond, body, (jnp.int32(1), c, acc))
    for h in range(H_SB):
        hs = slice(h * DH_SB, (h + 1) * DH_SB)
        o_ref[:, hs] = acc[h * n_tok:(h + 1) * n_tok, hs]


def _block_diag_queries(q, groups):
    B, T, W = q.shape
    d = W // groups
    gid_col = (jnp.arange(W) // d)[None, None, None, :]
    gid_row = jnp.arange(groups)[None, :, None, None]
    out = jnp.where(gid_col == gid_row, q[:, None, :, :], jnp.zeros((), q.dtype))
    return out.reshape(B, groups * T, W)


def _pad_rows(x, n):
    return jnp.pad(x, ((0, 0), (0, n - x.shape[1]), (0, 0)))


_NEW_PAD = 16


def _sb_sample_call(page_table, q, k_new, v_new, cache_k, cache_v, layer, P=4):
    B, n_tok, W = q.shape
    qbd = _block_diag_queries(q, H_SB)
    rows = qbd.shape[1]
    kn, vn = _pad_rows(k_new, _NEW_PAD), _pad_rows(v_new, _NEW_PAD)
    grid_spec = pltpu.PrefetchScalarGridSpec(
        num_scalar_prefetch=1,
        grid=(B,),
        in_specs=[pl.BlockSpec((None, rows, W), lambda b, pt: (b, 0, 0)),
                  pl.BlockSpec((None, _NEW_PAD, W), lambda b, pt: (b, 0, 0)),
                  pl.BlockSpec((None, _NEW_PAD, W), lambda b, pt: (b, 0, 0)),
                  pl.BlockSpec((PAGE, PAGE), lambda b, pt: (0, 0)),
                  pl.BlockSpec((_NEW_PAD, _NEW_PAD), lambda b, pt: (0, 0)),
                  pl.BlockSpec(memory_space=pl.ANY),
                  pl.BlockSpec(memory_space=pl.ANY)],
        out_specs=pl.BlockSpec((None, n_tok, W), lambda b, pt: (b, 0, 0)),
        scratch_shapes=[pltpu.VMEM((3, P, PAGE, W), F32), pltpu.VMEM((3, P, PAGE, W), F32),
                        pltpu.SemaphoreType.DMA((3,)), pltpu.SemaphoreType.DMA((3,))])
    return pl.pallas_call(
        functools.partial(_sb_sample_kernel, layer=layer, P=P, n_tok=n_tok),
        grid_spec=grid_spec,
        out_shape=jax.ShapeDtypeStruct((B, n_tok, W), F32),
        compiler_params=_cparams(("arbitrary",)),
        name="sb_sample",
    )(page_table, qbd, kn, vn, _suffix_matrix(PAGE), _suffix_matrix(_NEW_PAD), cache_k, cache_v)


def _df_sample_kernel(pt_ref, lam_ref, q_ref, kn_ref, vn_ref, ck_ref, cv_ref, o_ref,
                      kbuf, vbuf, semk, semv, m_scr, l_scr, acc_scr, *, layer, P, n_tok):
    g = pl.program_id(0)
    total = pl.num_programs(0)
    n_pages = pt_ref.shape[1]
    nch = n_pages // P
    c = g % nch
    slot = g % 2

    def copies(gg, sl):
        bb = gg // nch
        cc = gg % nch
        pages = [pt_ref[bb, cc * P + j] for j in range(P)]
        return _page_copies(ck_ref, cv_ref, kbuf, vbuf, semk, semv, layer, pages, sl)

    @pl.when(g == 0)
    def _():
        for cp in copies(0, 0):
            cp.start()

    @pl.when(g + 1 < total)
    def _():
        for cp in copies(g + 1, 1 - slot):
            cp.start()

    q = q_ref[...]
    rows = q.shape[0]

    @pl.when(c == 0)
    def _():
        nn = kn_ref.shape[0]
        t_of_row = lax.broadcasted_iota(jnp.int32, (rows, nn), 0) % n_tok
        j_new = lax.broadcasted_iota(jnp.int32, (rows, nn), 1)
        s = jnp.where(j_new <= t_of_row, _mm_nt(q, kn_ref[...]), -jnp.inf)
        m = jnp.max(s, axis=1, keepdims=True)
        p = jnp.exp(s - m)
        m_scr[...] = m
        l_scr[...] = jnp.sum(p, axis=1, keepdims=True)
        acc_scr[...] = _mm(p.astype(BF16), vn_ref[...])

    for cp in copies(g, slot):
        cp.wait()
    k = kbuf[slot].reshape(P * PAGE, kbuf.shape[-1]).astype(BF16)
    v = vbuf[slot].reshape(P * PAGE, vbuf.shape[-1]).astype(BF16)
    m, l, acc = _softmax_step(_mm_nt(q, k), v, m_scr[...], l_scr[...], acc_scr[...])
    m_scr[...] = m
    l_scr[...] = l
    acc_scr[...] = acc

    @pl.when(c == nch - 1)
    def _():
        o = acc / l
        for h in range(H_DF):
            hs = slice(h * DV_DF, (h + 1) * DV_DF)
            r0 = h * 2 * n_tok
            o_ref[:, hs] = o[r0:r0 + n_tok, hs] - lam_ref[0, 0] * o[r0 + n_tok:r0 + 2 * n_tok, hs]


def _df_sample_call(page_table, lam, q, k_new, v_new, cache_k, cache_v, layer, P=16):
    B, n_tok, W = q.shape
    qbd = _block_diag_queries(q, 2 * H_DF)
    rows = qbd.shape[1]
    kn, vn = _pad_rows(k_new, _NEW_PAD), _pad_rows(v_new, _NEW_PAD)
    n_pages = page_table.shape[1]
    nch = n_pages // P
    grid_spec = pltpu.PrefetchScalarGridSpec(
        num_scalar_prefetch=1,
        grid=(B * nch,),
        in_specs=[pl.BlockSpec(memory_space=pltpu.SMEM),
                  pl.BlockSpec((None, rows, W), lambda g, pt: (g // nch, 0, 0)),
                  pl.BlockSpec((None, _NEW_PAD, W), lambda g, pt: (g // nch, 0, 0)),
                  pl.BlockSpec((None, _NEW_PAD, W_DF), lambda g, pt: (g // nch, 0, 0)),
                  pl.BlockSpec(memory_space=pl.ANY),
                  pl.BlockSpec(memory_space=pl.ANY)],
        out_specs=pl.BlockSpec((None, n_tok, W_DF), lambda g, pt: (g // nch, 0, 0)),
        scratch_shapes=[pltpu.VMEM((2, P, PAGE, W), F32), pltpu.VMEM((2, P, PAGE, W_DF), F32),
                        pltpu.SemaphoreType.DMA((2,)), pltpu.SemaphoreType.DMA((2,)),
                        pltpu.VMEM((rows, 1), F32), pltpu.VMEM((rows, 1), F32), pltpu.VMEM((rows, W_DF), F32)])
    return pl.pallas_call(
        functools.partial(_df_sample_kernel, layer=layer, P=P, n_tok=n_tok),
        grid_spec=grid_spec,
        out_shape=jax.ShapeDtypeStruct((B, n_tok, W_DF), F32),
        compiler_params=_cparams(("arbitrary",)),
        name="df_sample",
    )(page_table, lam, qbd, kn, vn, cache_k, cache_v)


def _group_mean_matrix(width, d):
    g = np.arange(width) // d
    return jnp.asarray((g[:, None] == g[None, :]).astype(np.float32) / d, BF16)


def _head_norm(o, group_mean, gain):
    return o * lax.rsqrt(_mm_f32_lhs(o * o, group_mean) + EPS) * gain


def _rms_mod(x, g, sc, sh):
    y = x * lax.rsqrt(jnp.mean(x * x, axis=-1, keepdims=True) + EPS) * g
    return y * (1.0 + sc) + sh


def _merge_kernel(x_ref, osb_ref, ohg_ref, ghg_ref, odf_ref, gsb_ref, gnh_ref, gdf_ref, m64_ref, m128_ref,
                  w_ref, g1_ref, n2_ref, sc2_ref, sh2_ref, x_out, h2_out, *, df_scale):
    m64 = m64_ref[...]
    gate = ghg_ref[...]
    o_sb = _head_norm(osb_ref[...], m64, gsb_ref[...]).astype(BF16)
    o_hg = (_head_norm(ohg_ref[...], m64, gnh_ref[...]) * (gate / (1.0 + jnp.exp(-gate)))).astype(BF16)
    o_df = (_head_norm(odf_ref[...], m128_ref[...], gdf_ref[...]) * df_scale).astype(BF16)
    proj = (_mm(o_sb, w_ref[0:W_SB, :]) + _mm(o_hg, w_ref[W_SB:W_SB + W_HG, :])
            + _mm(o_df, w_ref[W_SB + W_HG:, :]))
    x = x_ref[...] + g1_ref[...] * proj
    x_out[...] = x
    h2_out[...] = _rms_mod(x, n2_ref[...], sc2_ref[...], sh2_ref[...]).astype(BF16)


def _merge_call(x, o_sb, o_hg, gate, o_df, gn_sb, gn_hg, gn_df, w_out_b, mod, norm2_g, df_scale, tm):
    G, T, D = x.shape
    R = mod.shape[1]
    rb = 1 if R == 1 else tm
    mrow = (lambda comp: (lambda b, i: (b, 0 if R == 1 else i, comp)))
    tok = lambda w: pl.BlockSpec((None, tm, w), lambda b, i: (b, i, 0))
    full = lambda a: pl.BlockSpec(a.shape, lambda b, i: (0,) * a.ndim)
    m64, m128 = _group_mean_matrix(W_SB, DH_SB), _group_mean_matrix(W_DF, DV_DF)
    gsb, ghg, gdf, n2 = gn_sb.reshape(1, -1), gn_hg.reshape(1, -1), gn_df.reshape(1, -1), norm2_g.reshape(1, D)
    return pl.pallas_call(
        functools.partial(_merge_kernel, df_scale=df_scale),
        grid=(G, T // tm),
        in_specs=[tok(D), tok(W_SB), tok(W_HG), tok(W_HG), tok(W_DF), full(gsb), full(ghg), full(gdf),
                  full(m64), full(m128), full(w_out_b),
                  pl.BlockSpec((None, rb, D), mrow(2)), full(n2),
                  pl.BlockSpec((None, rb, D), mrow(4)), pl.BlockSpec((None, rb, D), mrow(3))],
        out_specs=[tok(D), tok(D)],
        out_shape=[jax.ShapeDtypeStruct((G, T, D), F32), jax.ShapeDtypeStruct((G, T, D), BF16)],
        compiler_params=_cparams(("parallel", "parallel")),
        name="merge_outproj",
    )(x, o_sb, o_hg, gate, o_df, gsb, ghg, gdf, m64, m128, w_out_b, mod, n2, mod, mod)


def _final_norm_kernel(x_ref, g_ref, o_ref):
    x = x_ref[...]
    o_ref[...] = x * lax.rsqrt(jnp.mean(x * x, axis=-1, keepdims=True) + EPS) * g_ref[...]


def _final_norm_call(x, g, tm):
    G, T, D = x.shape
    return pl.pallas_call(
        _final_norm_kernel,
        grid=(G, T // tm),
        in_specs=[pl.BlockSpec((None, tm, D), lambda b, i: (b, i, 0)), pl.BlockSpec((1, D), lambda b, i: (0, 0))],
        out_specs=pl.BlockSpec((None, tm, D), lambda b, i: (b, i, 0)),
        out_shape=jax.ShapeDtypeStruct((G, T, D), F32),
        compiler_params=_cparams(("parallel", "parallel")),
        name="final_norm",
    )(x, g.reshape(1, D))


def _top16(s):
    n, L = s.shape
    iota = lax.broadcasted_iota(jnp.int32, (n, L), 0)
    row16 = lax.broadcasted_iota(jnp.int32, (PEER_TOPK, L), 0)

    def body(j, carry):
        s, rank, vals = carry
        m = jnp.max(s, axis=0, keepdims=True)
        idx = jnp.min(jnp.where(s == m, iota, n), axis=0, keepdims=True)
        sel = iota == idx
        rank = jnp.where(sel, lax.convert_element_type(j, F32), rank)
        s = jnp.where(sel, -jnp.inf, s)
        vals = jnp.where(row16 == j, m, vals)
        return s, rank, vals

    init = (s, jnp.full((n, L), float(PEER_TOPK), F32), jnp.zeros((PEER_TOPK, L), F32))
    _, rank, vals = lax.fori_loop(0, PEER_TOPK, body, init)
    return rank, vals


_CAND_ROWS = (16,) + (8,) * 7


def _peer_route_kernel(h2_ref, wq_ref, k1_ref, k2_ref, rank2_ref, e2_ref, cnt_ref, e1_ref, qt_scr):
    qt_scr[...] = _mm_nt(wq_ref[...], h2_ref[...])
    L = qt_scr.shape[1]
    dk = k1_ref.shape[2]
    ncand = sum(_CAND_ROWS) + 8
    iota_c = lax.broadcasted_iota(jnp.int32, (ncand, L), 0)
    row8 = lax.broadcasted_iota(jnp.int32, (8, L), 0)

    def head(h, _):
        base = pl.multiple_of(h * 2 * dk, 2 * dk)
        s1 = _mm(k1_ref[h], qt_scr[pl.ds(base, dk), :].astype(BF16))
        s2 = _mm(k2_ref[h], qt_scr[pl.ds(base + dk, dk), :].astype(BF16))
        rank1, v1 = _top16(s1)
        rank2, v2 = _top16(s2)
        cand = jnp.concatenate(
            [v1[r:r + 1] + v2[0:n] for r, n in enumerate(_CAND_ROWS)] + [v1[8:16] + v2[0:1]], axis=0)
        m0 = v1[0:1] + v2[0:1]

        def pick(j, carry):
            cand, chosen, z = carry
            m = jnp.max(cand, axis=0, keepdims=True)
            idx = jnp.min(jnp.where(cand == m, iota_c, ncand), axis=0, keepdims=True)
            sel = iota_c == idx
            return jnp.where(sel, -jnp.inf, cand), jnp.where(sel, 1.0, chosen), z + jnp.exp(m - m0)

        _, chosen, z = lax.fori_loop(0, PEER_TOPK, pick,
                                     (cand, jnp.zeros((ncand, L), F32), jnp.zeros((1, L), F32)))
        low = jnp.zeros((8, L), F32)
        off = 0
        for r, n in enumerate(_CAND_ROWS):
            low = jnp.where(row8 == r, jnp.sum(chosen[off:off + n], axis=0, keepdims=True), low)
            off += n
        cnt16 = jnp.concatenate([low, chosen[off:off + 8]], axis=0)
        cnt = jnp.zeros_like(s1)
        for r in range(PEER_TOPK):
            cnt = jnp.where(rank1 == float(r), cnt16[r:r + 1], cnt)
        rank2_ref[h] = rank2
        e2_ref[h] = jnp.exp(s2 - v2[0:1])
        cnt_ref[h] = cnt
        e1_ref[h] = jnp.exp(s1 - v1[0:1]) / z
        return 0

    lax.fori_loop(0, k1_ref.shape[0], head, 0)


def _peer_route_call(h2, wq_t, k1, k2):
    N, D = h2.shape
    heads, keys, dk = k1.shape
    tn = LANES
    tbl = pl.BlockSpec((heads, keys, tn), lambda i: (0, 0, i))
    shp = jax.ShapeDtypeStruct((heads, keys, N), F32)
    return pl.pallas_call(
        _peer_route_kernel,
        grid=(N // tn,),
        in_specs=[pl.BlockSpec((tn, D), lambda i: (i, 0)),
                  pl.BlockSpec(wq_t.shape, lambda i: (0, 0)),
                  pl.BlockSpec(k1.shape, lambda i: (0, 0, 0)),
                  pl.BlockSpec(k2.shape, lambda i: (0, 0, 0))],
        out_specs=[tbl, tbl, tbl, tbl],
        out_shape=[shp, shp, shp, shp],
        scratch_shapes=[pltpu.VMEM((wq_t.shape[0], tn), F32)],
        compiler_params=_cparams(("parallel",)),
        name="peer_route",
    )(h2, wq_t, k1, k2)


def _peer_dense_kernel(x_ref, g2_ref, h2_ref, u_ref, vt_ref, rank2_ref, e2_ref, cnt_ref, e1_ref, o_ref, acc_scr,
                       *, te):
    et = pl.program_id(2)
    heads, keys, _ = rank2_ref.shape

    @pl.when(et == 0)
    def _():
        acc_scr[...] = jnp.zeros_like(acc_scr)

    a = _mm_nt(u_ref[...], h2_ref[...])
    act = 0.5 * a * (1.0 + lax.erf(a * (2.0 ** -0.5)))
    pieces = []
    for j in range(te // keys):
        i1 = et * (te // keys) + j
        g = jnp.zeros((keys, a.shape[1]), F32)
        for h in range(heads):
            cnt_row = cnt_ref[h, pl.ds(i1, 1), :]
            e1_row = e1_ref[h, pl.ds(i1, 1), :]
            g = g + jnp.where(rank2_ref[h] < cnt_row, e1_row * e2_ref[h], 0.0)
        pieces.append(g)
    gate = jnp.concatenate(pieces, axis=0)
    acc_scr[...] += _mm(vt_ref[...], (gate * act).astype(BF16))

    @pl.when(et == pl.num_programs(2) - 1)
    def _():
        o_ref[...] = x_ref[...] + g2_ref[...] * acc_scr[...].T


def _peer_dense_call(x, mod, h2, u_b, vt_b, tables, tn, te):
    G, T, D = x.shape
    E = u_b.shape[0]
    R = mod.shape[1]
    rb = 1 if R == 1 else tn
    nt = T // tn
    heads, keys, _ = tables[0].shape
    tbl = pl.BlockSpec((heads, keys, tn), lambda b, i, e: (0, 0, b * nt + i))
    tok = pl.BlockSpec((None, tn, D), lambda b, i, e: (b, i, 0))
    return pl.pallas_call(
        functools.partial(_peer_dense_kernel, te=te),
        grid=(G, nt, E // te),
        in_specs=[tok, pl.BlockSpec((None, rb, D), lambda b, i, e: (b, 0 if R == 1 else i, 5)), tok,
                  pl.BlockSpec((te, D), lambda b, i, e: (e, 0)),
                  pl.BlockSpec((D, te), lambda b, i, e: (0, e)),
                  tbl, tbl, tbl, tbl],
        out_specs=tok,
        out_shape=jax.ShapeDtypeStruct((G, T, D), F32),
        scratch_shapes=[pltpu.VMEM((D, tn), F32)],
        compiler_params=_cparams(("parallel", "parallel", "arbitrary")),
        name="peer_dense",
    )(x, mod, h2, u_b, vt_b, *tables)


def _layer(l, x, mod, page_table, caches, s0T, lb_row, lam_total, lam_init, lw, sizes):
    tm, tq_sb, tq_df, chunk, tn_peer, te_peer = sizes
    G, T, D = x.shape
    z = _inproj_call(x, lw["norm1_g"], mod, lw["w_in"], tm)
    if caches is None:
        o_sb = _sb_prompt_call(z["qsb"], z["ksb_b"], z["vsb_b"], tq_sb)
        o_df = _df_prompt_call(lam_total, z["qdf"], z["kdf_b"], z["vdf_b"], tq_df)
        tok = lambda a: a
    else:
        B = page_table.shape[0]
        per = lambda a: a.reshape(B, T // B, a.shape[-1])
        tok = lambda a: a.reshape(1, T, a.shape[-1])
        ck_sb, cv_sb, ck_df, cv_df = caches
        o_sb = tok(_sb_sample_call(page_table, per(z["qsb"]), per(z["ksb_b"]), per(z["vsb_b"]), ck_sb, cv_sb, l))
        o_df = tok(_df_sample_call(page_table, lam_total, per(z["qdf"]), per(z["kdf_b"]), per(z["vdf_b"]),
                                   ck_df, cv_df, l))
        z = dict(z, qhg=per(z["qhg"]), ahg=per(z["ahg"]), ihg=per(z["ihg"]))
    o_hg, sT = _hgrn_call(z["qhg"], z["ahg"], z["ihg"], lb_row, s0T, chunk)
    x, h2 = _merge_call(x, o_sb, tok(o_hg), z["ghg"], o_df, lw["gn_sb"], lw["gn_hg"], lw["gn_df"], lw["w_out"],
                        mod, lw["norm2_g"], 1.0 - lam_init, tm)
    tables = _peer_route_call(h2.reshape(G * T, D), lw["w_pq_t"], lw["peer_k1"], lw["peer_k2"])
    x = _peer_dense_call(x, mod, h2, lw["peer_u"], lw["peer_v_t"], tables, tn_peer, te_peer)
    return x, (z["ksb"], z["vsb"], z["kdf"], z["vdf"], sT)


def kernel(x_prompt, x_sample, cache_sb_k, cache_sb_v, cache_df_k, cache_df_v, state_hgrn, page_table,
           c_prompt, c_sample, w_ada, b_ada, norm1_g, norm2_g, w_in, lb_logits, lam_q1, lam_k1, lam_q2, lam_k2,
           gn_sb, gn_hg, gn_df, w_out, w_pq, peer_k1, peer_k2, peer_u, peer_v, final_g):
    depth = w_ada.shape[0]
    Bp, Tp, D = x_prompt.shape
    Bs, Ts, _ = x_sample.shape
    n_pool = cache_sb_k.shape[1]

    c_all = jnp.concatenate([c_prompt, c_sample], axis=0)
    pad = (-c_all.shape[0]) % 8
    mod_all = _mod_call(jnp.pad(c_all, ((0, pad), (0, 0))), w_ada, b_ada)
    lb_all, lam_diff = _prep_call(lb_logits, lam_q1, lam_k1, lam_q2, lam_k2)

    caches = (cache_sb_k.reshape(depth, n_pool, PAGE, W_SB), cache_sb_v.reshape(depth, n_pool, PAGE, W_SB),
              cache_df_k.reshape(depth, n_pool, PAGE, NQK_DF), cache_df_v.reshape(depth, n_pool, PAGE, W_DF))
    s0T_sample = jnp.swapaxes(state_hgrn, 3, 4)
    s0T_prompt = jnp.zeros((Bp, H_HG, DV_HG, DK_HG), F32)

    xp = x_prompt
    xs = x_sample.reshape(1, Bs * Ts, D)
    sizes_p = (512, 128, 256, 128, 512, 512)
    sizes_s = (Bs * Ts, None, None, Ts, Bs * Ts, 512)
    outs_p, outs_s = [], []
    for l in range(depth):
        lam_init = 0.8 - 0.6 * math.exp(-0.3 * l)
        lw = dict(norm1_g=norm1_g[l], norm2_g=norm2_g[l], w_in=w_in[l].astype(BF16), gn_sb=gn_sb[l], gn_hg=gn_hg[l],
                  gn_df=gn_df[l], w_out=w_out[l].astype(BF16), w_pq_t=w_pq[l].T.astype(BF16),
                  peer_k1=peer_k1[l].astype(BF16), peer_k2=peer_k2[l].astype(BF16),
                  peer_u=peer_u[l].astype(BF16), peer_v_t=peer_v[l].T.astype(BF16))
        lam_total = (lam_diff[l, 0] + lam_init).reshape(1, 1)
        lb_row = lb_all[l].reshape(1, NK_HG)
        mod_p = mod_all[l, :Bp].reshape(Bp, 1, 6 * D)
        mod_s = jnp.repeat(mod_all[l, Bp:Bp + Bs], Ts, axis=0).reshape(1, Bs * Ts, 6 * D)
        xs, os_ = _layer(l, xs, mod_s, page_table, caches, s0T_sample[l], lb_row, lam_total, lam_init, lw, sizes_s)
        xp, op = _layer(l, xp, mod_p, None, None, s0T_prompt, lb_row, lam_total, lam_init, lw, sizes_p)
        outs_p.append(op)
        outs_s.append(os_)

    y_prompt = _final_norm_call(xp, final_g, 512)
    y_sample = _final_norm_call(xs, final_g, Bs * Ts).reshape(Bs, Ts, D)

    def stack(outs, B, T):
        ksb = jnp.stack([o[0] for o in outs]).reshape(depth, B, T, H_SB, DH_SB)
        vsb = jnp.stack([o[1] for o in outs]).reshape(depth, B, T, H_SB, DH_SB)
        kdf = jnp.stack([o[2] for o in outs]).reshape(depth, B, T, H_DF, 2 * DH_DF)
        vdf = jnp.stack([o[3] for o in outs]).reshape(depth, B, T, H_DF, DV_DF)
        st = jnp.swapaxes(jnp.stack([o[4] for o in outs]), 3, 4)
        return ksb, vsb, kdf, vdf, st

    return (y_prompt, y_sample) + stack(outs_p, Bp, Tp) + stack(outs_s, Bs, Ts)
```

```python
import functools
import math

import numpy as np
import jax
import jax.numpy as jnp
from jax import lax
from jax.experimental import pallas as pl
from jax.experimental.pallas import tpu as pltpu

F32 = jnp.float32
BF16 = jnp.bfloat16
EPS = 1e-6
LANES = 128
VMEM_LIMIT = 48 * 1024 * 1024
SB_DEAD = -120.0

H_SB, DH_SB = 4, 64
H_HG, DK_HG, DV_HG = 4, 128, 64
H_DF, DH_DF, DV_DF = 4, 64, 128
W_SB, W_HG, W_DF = H_SB * DH_SB, H_HG * DV_HG, H_DF * DV_DF
NK_HG = H_HG * DK_HG
NQK_DF = H_DF * 2 * DH_DF
PEER_HEADS, PEER_NKEYS, PEER_TOPK = 8, 128, 16
PAGE = 128


def _cparams(sem):
    return pltpu.CompilerParams(dimension_semantics=sem, vmem_limit_bytes=VMEM_LIMIT)


def _mm(a, b):
    return lax.dot_general(a, b, (((1,), (0,)), ((), ())), preferred_element_type=F32)


def _mm_nt(a, b):
    return lax.dot_general(a, b, (((1,), (1,)), ((), ())), preferred_element_type=F32)


def _mm_tn(a, b):
    return lax.dot_general(a, b, (((0,), (0,)), ((), ())), preferred_element_type=F32)


def _split(x):
    hi = x.astype(BF16)
    lo = (x - hi.astype(F32)).astype(BF16)
    return hi, lo


def _mm_f32_lhs(x, m_bf16):
    hi, lo = _split(x)
    return _mm(hi, m_bf16) + _mm(lo, m_bf16)


def _mm_f32_rhs(m_bf16, x):
    hi, lo = _split(x)
    return _mm(m_bf16, hi) + _mm(m_bf16, lo)


def _log_sigmoid_pair(z):
    l1p = jnp.log1p(jnp.exp(-jnp.abs(z)))
    return jnp.minimum(z, 0.0) - l1p, -(jnp.maximum(z, 0.0) + l1p)


def _mod_kernel(c_ref, w_ref, b_ref, o_ref):
    c = c_ref[...]
    a = c * (1.0 / (1.0 + jnp.exp(-c)))
    ahi, alo = _split(a)
    w = w_ref[...]
    whi, wlo = _split(w)
    o_ref[...] = _mm(ahi, whi) + _mm(ahi, wlo) + _mm(alo, whi) + b_ref[...]


def _mod_call(c_all, w_ada, b_ada):
    depth, d, d6 = w_ada.shape
    r = c_all.shape[0]
    tn = 1024
    return pl.pallas_call(
        _mod_kernel,
        grid=(depth, d6 // tn),
        in_specs=[pl.BlockSpec((r, d), lambda l, j: (0, 0)),
                  pl.BlockSpec((None, d, tn), lambda l, j: (l, 0, j)),
                  pl.BlockSpec((None, 1, tn), lambda l, j: (l, 0, j))],
        out_specs=pl.BlockSpec((None, r, tn), lambda l, j: (l, 0, j)),
        out_shape=jax.ShapeDtypeStruct((depth, r, d6), F32),
        compiler_params=_cparams(("parallel", "parallel")),
        name="adaln_mod",
    )(c_all, w_ada, b_ada.reshape(depth, 1, d6))


def _prep_kernel(lb_ref, q1_ref, k1_ref, q2_ref, k2_ref, lbo_ref, lam_ref):
    lg = lb_ref[...]
    depth = lg.shape[0]
    mx = jnp.max(lg, axis=0, keepdims=True)
    e = jnp.exp(lg - mx)
    p = e / jnp.sum(e, axis=0, keepdims=True)
    run = p[0:1]
    rows = [jnp.zeros_like(run)]
    for l in range(1, depth):
        run = run + p[l:l + 1]
        rows.append(run - p[0:1])
    lbo_ref[...] = jnp.concatenate(rows, axis=0)
    s1 = jnp.sum(q1_ref[...] * k1_ref[...], axis=1, keepdims=True)
    s2 = jnp.sum(q2_ref[...] * k2_ref[...], axis=1, keepdims=True)
    lam_ref[...] = jnp.broadcast_to(jnp.exp(s1) - jnp.exp(s2), lam_ref.shape)


def _prep_call(lb_logits, lq1, lk1, lq2, lk2):
    depth, n = lb_logits.shape
    return pl.pallas_call(
        _prep_kernel,
        out_shape=(jax.ShapeDtypeStruct((depth, n), F32),
                   jax.ShapeDtypeStruct((depth, LANES), F32)),
        name="layer_scalars",
    )(lb_logits, lq1, lk1, lq2, lk2)


_IN_SEGS = (
    ("qsb", W_SB, BF16, DH_SB ** -0.5), ("ksb", W_SB, F32, None), ("vsb", W_SB, F32, None),
    ("qhg", NK_HG, F32, None), ("ahg", NK_HG, F32, None), ("ihg", W_HG, BF16, None), ("ghg", W_HG, F32, None),
    ("qdf", NQK_DF, BF16, DH_DF ** -0.5 * math.log2(math.e)), ("kdf", NQK_DF, F32, None), ("vdf", W_DF, F32, None),
)
_BF16_COPIES = ("ksb", "vsb", "kdf", "vdf")
_T_COPIES = ("qdf", "vdf")
_IN_NAMES = [s[0] for s in _IN_SEGS] + [n + "_b" for n in _BF16_COPIES] + [n + "_t" for n in _T_COPIES]


def _inproj_kernel(x_ref, g_ref, sc_ref, sh_ref, w_ref, *out_refs):
    x = x_ref[...]
    y = x * lax.rsqrt(jnp.mean(x * x, axis=-1, keepdims=True) + EPS) * g_ref[...]
    h = (y * (1.0 + sc_ref[...]) + sh_ref[...]).astype(BF16)
    outs = dict(zip(_IN_NAMES, out_refs))
    off = 0
    for name, width, dt, scale in _IN_SEGS:
        z = _mm(h, w_ref[:, off:off + width])
        off += width
        if scale is not None:
            z = z * scale
        outs[name][...] = z.astype(dt)
        if name in _BF16_COPIES:
            outs[name + "_b"][...] = z.astype(BF16)
        if name in _T_COPIES:
            outs[name + "_t"][...] = z.T.astype(BF16)


def _inproj_call(x, g, mod, w_in_b, tm):
    G, T, D = x.shape
    R = mod.shape[1]
    rb = 1 if R == 1 else tm
    nt = T // tm
    width = w_in_b.shape[1]
    mrow = (lambda comp: (lambda b, i: (b, 0 if R == 1 else i, comp)))
    names = _IN_NAMES
    widths = {s[0]: s[1] for s in _IN_SEGS}
    dts = {s[0]: s[2] for s in _IN_SEGS}
    for n in _BF16_COPIES:
        widths[n + "_b"], dts[n + "_b"] = widths[n], BF16

    def out_spec(n):
        if n.endswith("_t"):
            return pl.BlockSpec((None, widths[n[:-2]], tm), lambda b, i: (b, 0, i))
        return pl.BlockSpec((None, tm, widths[n]), lambda b, i: (b, i, 0))

    def out_shape(n):
        if n.endswith("_t"):
            return jax.ShapeDtypeStruct((G, widths[n[:-2]], T), BF16)
        return jax.ShapeDtypeStruct((G, T, widths[n]), dts[n])

    outs = pl.pallas_call(
        _inproj_kernel,
        grid=(G, nt),
        in_specs=[pl.BlockSpec((None, tm, D), lambda b, i: (b, i, 0)),
                  pl.BlockSpec((1, D), lambda b, i: (0, 0)),
                  pl.BlockSpec((None, rb, D), mrow(1)),
                  pl.BlockSpec((None, rb, D), mrow(0)),
                  pl.BlockSpec((D, width), lambda b, i: (0, 0))],
        out_specs=[out_spec(n) for n in names],
        out_shape=[out_shape(n) for n in names],
        compiler_params=_cparams(("parallel", "parallel")),
        name="norm_inproj",
    )(x, g.reshape(1, D), mod, mod, w_in_b)
    return dict(zip(names, outs))


def _sb_block(q, k, v, c, acc, u, mask, transposed=False):
    z = _mm(q, k) if transposed else _mm_nt(q, k)
    ls_pos, ls_neg = _log_sigmoid_pair(z)
    if mask is not None:
        ls_neg = jnp.where(mask, ls_neg, 0.0)
    surv = _mm_f32_lhs(ls_neg, u) + c
    w = jnp.exp(ls_pos + surv)
    if mask is not None:
        w = jnp.where(mask, w, 0.0)
    w = w.astype(BF16)
    acc = acc + (_mm_nt(w, v) if transposed else _mm(w, v))
    c = c + jnp.sum(ls_neg, axis=1, keepdims=True)
    return c, acc


def _sb_prompt_kernel(q_ref, k_ref, v_ref, u_ref, o_ref, *, tq):
    qi = pl.program_id(1)
    u = u_ref[...]
    row = lax.broadcasted_iota(jnp.int32, (tq, tq), 0)
    col = lax.broadcasted_iota(jnp.int32, (tq, tq), 1)
    diag_mask = col < row
    for h in range(H_SB):
        hs = slice(h * DH_SB, (h + 1) * DH_SB)
        q = q_ref[:, hs]
        start = pl.multiple_of(qi * tq, tq)
        c, acc = _sb_block(q, k_ref[pl.ds(start, tq), hs], v_ref[pl.ds(start, tq), hs],
                           jnp.zeros((tq, 1), F32), jnp.zeros((tq, DH_SB), F32), u, diag_mask)

        def cond(carry):
            kb, c, _ = carry
            return jnp.logical_and(kb >= 0, jnp.max(c) > SB_DEAD)

        def body(carry):
            kb, c, acc = carry
            st = pl.multiple_of(kb * tq, tq)
            c, acc = _sb_block(q, k_ref[pl.ds(st, tq), hs], v_ref[pl.ds(st, tq), hs], c, acc, u, None)
            return kb - 1, c, acc

        _, _, acc = lax.while_loop(cond, body, (qi - 1, c, acc))
        o_ref[:, hs] = acc


def _suffix_matrix(n):
    return jnp.asarray(np.tril(np.ones((n, n), np.float32), -1), BF16)


def _sb_prompt_call(q, k, v, tq):
    B, T, W = q.shape
    return pl.pallas_call(
        functools.partial(_sb_prompt_kernel, tq=tq),
        grid=(B, T // tq),
        in_specs=[pl.BlockSpec((None, tq, W), lambda b, i: (b, i, 0)),
                  pl.BlockSpec((None, T, W), lambda b, i: (b, 0, 0)),
                  pl.BlockSpec((None, T, W), lambda b, i: (b, 0, 0)),
                  pl.BlockSpec((tq, tq), lambda b, i: (0, 0))],
        out_specs=pl.BlockSpec((None, tq, W), lambda b, i: (b, i, 0)),
        out_shape=jax.ShapeDtypeStruct((B, T, W), F32),
        compiler_params=_cparams(("parallel", "parallel")),
        name="sb_prompt",
    )(q, k, v, _suffix_matrix(tq))


def _softmax_step(s, v, m, l, acc):
    m_new = jnp.maximum(m, jnp.max(s, axis=1, keepdims=True))
    alpha = jnp.exp2(m - m_new)
    p = jnp.exp2(s - m_new)
    l = alpha * l + jnp.sum(p, axis=1, keepdims=True)
    acc = alpha * acc + _mm(p.astype(BF16), v)
    return m_new, l, acc


def _df_prompt_kernel(lam_ref, qt_ref, k_ref, vt_ref, o_ref, *, tq):
    qi = pl.program_id(2)
    qt = qt_ref[...]
    first = lax.broadcasted_iota(jnp.int32, qt.shape, 0) < DH_DF
    zero = jnp.zeros((), BF16)
    qbd = jnp.concatenate([jnp.where(first, qt, zero), jnp.where(first, zero, qt)], axis=1)

    def scores(kb):
        return _mm(k_ref[pl.ds(pl.multiple_of(kb * tq, tq), tq), :], qbd)

    def update(kb, s, carry):
        m, l, acc = carry
        m_new = jnp.maximum(m, jnp.max(s, axis=0, keepdims=True))
        alpha = jnp.exp2(m - m_new)
        p = jnp.exp2(s - m_new)
        l = alpha * l + jnp.sum(p, axis=0, keepdims=True)
        acc = alpha * acc + _mm(vt_ref[:, pl.ds(pl.multiple_of(kb * tq, tq), tq)], p.astype(BF16))
        return m_new, l, acc

    key = lax.broadcasted_iota(jnp.int32, (tq, 2 * tq), 0)
    qry = lax.broadcasted_iota(jnp.int32, (tq, 2 * tq), 1) % tq
    init = (jnp.full((1, 2 * tq), -jnp.inf, F32), jnp.zeros((1, 2 * tq), F32), jnp.zeros((DV_DF, 2 * tq), F32))
    carry = update(qi, jnp.where(key <= qry, scores(qi), -jnp.inf), init)

    def pair(j, carry):
        s0, s1 = scores(2 * j), scores(2 * j + 1)
        return update(2 * j + 1, s1, update(2 * j, s0, carry))

    carry = lax.fori_loop(0, qi // 2, pair, carry)
    _, l, acc = lax.cond(qi % 2 == 1, lambda c: update(qi - 1, scores(qi - 1), c), lambda c: c, carry)
    o = acc / l
    o_ref[...] = (o[:, :tq] - lam_ref[0, 0] * o[:, tq:]).T


def _df_prompt_call(lam, q_t, k, v_t, tq):
    B, T, _ = k.shape
    blk = 2 * DH_DF
    return pl.pallas_call(
        functools.partial(_df_prompt_kernel, tq=tq),
        grid=(B, H_DF, T // tq),
        in_specs=[pl.BlockSpec(memory_space=pltpu.SMEM),
                  pl.BlockSpec((None, blk, tq), lambda b, h, i: (b, h, i)),
                  pl.BlockSpec((None, T, blk), lambda b, h, i: (b, 0, h)),
                  pl.BlockSpec((None, DV_DF, T), lambda b, h, i: (b, h, 0))],
        out_specs=pl.BlockSpec((None, tq, DV_DF), lambda b, h, i: (b, i, h)),
        out_shape=jax.ShapeDtypeStruct((B, T, W_DF), F32),
        compiler_params=_cparams(("parallel", "parallel", "parallel")),
        name="df_prompt",
    )(lam, q_t, k, v_t)


def _hgrn_consts(C):
    levels = int(math.log2(C))
    t = np.arange(C)
    tril = (t[:, None] >= t[None, :]).astype(np.float32)
    lmat, rmat, masks = [], [], []
    for lv in range(levels):
        w = C >> (lv + 1)
        blk = t // (2 * w)
        second = (t % (2 * w)) >= w
        mid = blk * 2 * w + w - 1
        j = t[None, :]
        lmat.append(((j > mid[:, None]) & (j <= t[:, None]) & second[:, None]).astype(np.float32))
        rmat.append(((j > t[:, None]) & (j <= mid[:, None]) & (~second)[:, None]).astype(np.float32))
        masks.append(((blk[:, None] == blk[None, :]) & second[:, None] & (~second)[None, :]).astype(np.float32))
    masks.append(np.eye(C, dtype=np.float32))
    lr = np.concatenate([tril] + lmat + rmat, axis=0)
    return jnp.asarray(lr, BF16), jnp.asarray(np.stack(masks), F32), levels


def _hgrn_kernel(q_ref, a_ref, i_ref, lb_ref, s0_ref, lr_ref, mk_ref, o_ref, sT_ref, st_scr, *, C, levels):
    ci = pl.program_id(1)

    @pl.when(ci == 0)
    def _():
        st_scr[...] = s0_ref[...]

    a = a_ref[...]
    lb = lb_ref[...]
    lsa, _ = _log_sigmoid_pair(a)
    x0 = jnp.log(lb)
    y0 = jnp.log1p(-lb) + lsa
    log_f = jnp.maximum(x0, y0) + jnp.log1p(jnp.exp(-jnp.abs(x0 - y0)))
    kk = (1.0 - lb) / (1.0 + jnp.exp(a))
    sums = _mm_f32_rhs(lr_ref[...], log_f)
    b = sums[:C]
    q = q_ref[...]
    q_in = (q * jnp.exp(b)).astype(BF16)
    k_out = (kk * jnp.exp(b[C - 1:C] - b)).astype(BF16)
    qb = q.astype(BF16)
    kb = kk.astype(BF16)
    dec_l = jnp.exp(b[C - 1:C])
    for h in range(H_HG):
        ks = slice(h * DK_HG, (h + 1) * DK_HG)
        vs = slice(h * DV_HG, (h + 1) * DV_HG)
        att = mk_ref[levels] * _mm_nt(qb[:, ks], kb[:, ks])
        for lv in range(levels):
            dq = sums[(1 + lv) * C:(2 + lv) * C, ks]
            dk = sums[(1 + levels + lv) * C:(2 + levels + lv) * C, ks]
            ql = (q[:, ks] * jnp.exp(dq)).astype(BF16)
            kl = (kk[:, ks] * jnp.exp(dk)).astype(BF16)
            att = att + mk_ref[lv] * _mm_nt(ql, kl)
        iv = i_ref[:, vs]
        sT = st_scr[h]
        o_ref[:, vs] = _mm_nt(q_in[:, ks], sT.astype(BF16)) + _mm(att.astype(BF16), iv)
        st_scr[h] = sT * dec_l[:, ks] + _mm_tn(iv, k_out[:, ks])

    @pl.when(ci == pl.num_programs(1) - 1)
    def _():
        sT_ref[...] = st_scr[...]


def _hgrn_call(q, a, iv, lb_row, s0T, C):
    B, T, _ = q.shape
    lr, masks, levels = _hgrn_consts(C)
    nlr = lr.shape[0]
    return pl.pallas_call(
        functools.partial(_hgrn_kernel, C=C, levels=levels),
        grid=(B, T // C),
        in_specs=[pl.BlockSpec((None, C, NK_HG), lambda b, i: (b, i, 0)),
                  pl.BlockSpec((None, C, NK_HG), lambda b, i: (b, i, 0)),
                  pl.BlockSpec((None, C, W_HG), lambda b, i: (b, i, 0)),
                  pl.BlockSpec((1, NK_HG), lambda b, i: (0, 0)),
                  pl.BlockSpec((None, H_HG, DV_HG, DK_HG), lambda b, i: (b, 0, 0, 0)),
                  pl.BlockSpec((nlr, C), lambda b, i: (0, 0)),
                  pl.BlockSpec((levels + 1, C, C), lambda b, i: (0, 0, 0))],
        out_specs=[pl.BlockSpec((None, C, W_HG), lambda b, i: (b, i, 0)),
                   pl.BlockSpec((None, H_HG, DV_HG, DK_HG), lambda b, i: (b, 0, 0, 0))],
        out_shape=[jax.ShapeDtypeStruct((B, T, W_HG), F32),
                   jax.ShapeDtypeStruct((B, H_HG, DV_HG, DK_HG), F32)],
        scratch_shapes=[pltpu.VMEM((H_HG, DV_HG, DK_HG), F32)],
        compiler_params=_cparams(("parallel", "arbitrary")),
        name="hgrn2",
    )(q, a, iv, lb_row, s0T, lr, masks)


def _page_copies(cache_k, cache_v, kbuf, vbuf, semk, semv, layer, pages, slot):
    cps = []
    for j, page in enumerate(pages):
        cps.append(pltpu.make_async_copy(cache_k.at[layer, page], kbuf.at[slot, j], semk.at[slot]))
        cps.append(pltpu.make_async_copy(cache_v.at[layer, page], vbuf.at[slot, j], semv.at[slot]))
    return cps


def _sb_sample_kernel(pt_ref, q_ref, kn_ref, vn_ref, u_ref, un_ref, ck_ref, cv_ref, o_ref,
                      kbuf, vbuf, semk, semv, *, layer, P, n_tok):
    b = pl.program_id(0)
    nb = pl.num_programs(0)
    n_pages = pt_ref.shape[1]
    nch = n_pages // P
    slot = b % 2

    def copies(bb, ch, sl):
        pages = [pt_ref[bb, n_pages - 1 - ch * P - j] for j in range(P)]
        return _page_copies(ck_ref, cv_ref, kbuf, vbuf, semk, semv, layer, pages, sl)

    @pl.when(b == 0)
    def _():
        for cp in copies(0, 0, 0):
            cp.start()

    @pl.when(b + 1 < nb)
    def _():
        for cp in copies(b + 1, 0, 1 - slot):
            cp.start()

    q = q_ref[...]
    rows = q.shape[0]
    nn = kn_ref.shape[0]
    t_of_row = lax.broadcasted_iota(jnp.int32, (rows, nn), 0) % n_tok
    j_new = lax.broadcasted_iota(jnp.int32, (rows, nn), 1)
    mask_new = jnp.logical_and(j_new < t_of_row, j_new < n_tok)
    c, acc = _sb_block(q, kn_ref[...], vn_ref[...], jnp.zeros((rows, 1), F32),
                       jnp.zeros((rows, W_SB), F32), un_ref[...], mask_new)
    u = u_ref[...]

    def chunk(sl, c, acc):
        for j in range(P):
            kt = kbuf[sl, j].reshape(W_SB, PAGE).astype(BF16)
            vt = vbuf[sl, j].reshape(W_SB, PAGE).astype(BF16)
            c, acc = _sb_block(q, kt, vt, c, acc, u, None, transposed=True)
        return c, acc

    for cp in copies(b, 0, slot):
        cp.wait()
    c, acc = chunk(slot, c, acc)

    def cond(carry):
        ch, c, _ = carry
        return jnp.logical_and(ch < nch, jnp.max(c) > SB_DEAD)

    def body(carry):
        ch, c, acc = carry
        cps = copies(b, ch, 2)
        for cp in cps:
            cp.start()
        for cp in cps:
            cp.wait()
        c, acc = chunk(2, c, acc)
        return ch + 1, c, acc

    _, _, acc = lax.while_loop(cond, body, (jnp.int32(1), c, acc))
    for h in range(H_SB):
        hs = slice(h * DH_SB, (h + 1) * DH_SB)
        o_ref[:, hs] = acc[h * n_tok:(h + 1) * n_tok, hs]


def _block_diag_queries(q, groups):
    B, T, W = q.shape
    d = W // groups
    gid_col = (jnp.arange(W) // d)[None, None, None, :]
    gid_row = jnp.arange(groups)[None, :, None, None]
    out = jnp.where(gid_col == gid_row, q[:, None, :, :], jnp.zeros((), q.dtype))
    return out.reshape(B, groups * T, W)


def _pad_rows(x, n):
    return jnp.pad(x, ((0, 0), (0, n - x.shape[1]), (0, 0)))


_NEW_PAD = 16


def _sb_sample_call(page_table, q, k_new, v_new, cache_k, cache_v, layer, P=4):
    B, n_tok, W = q.shape
    page_shape = cache_k.shape[2:]
    qbd = _block_diag_queries(q, H_SB)
    rows = qbd.shape[1]
    kn, vn = _pad_rows(k_new, _NEW_PAD), _pad_rows(v_new, _NEW_PAD)
    grid_spec = pltpu.PrefetchScalarGridSpec(
        num_scalar_prefetch=1,
        grid=(B,),
        in_specs=[pl.BlockSpec((None, rows, W), lambda b, pt: (b, 0, 0)),
                  pl.BlockSpec((None, _NEW_PAD, W), lambda b, pt: (b, 0, 0)),
                  pl.BlockSpec((None, _NEW_PAD, W), lambda b, pt: (b, 0, 0)),
                  pl.BlockSpec((PAGE, PAGE), lambda b, pt: (0, 0)),
                  pl.BlockSpec((_NEW_PAD, _NEW_PAD), lambda b, pt: (0, 0)),
                  pl.BlockSpec(memory_space=pl.ANY),
                  pl.BlockSpec(memory_space=pl.ANY)],
        out_specs=pl.BlockSpec((None, n_tok, W), lambda b, pt: (b, 0, 0)),
        scratch_shapes=[pltpu.VMEM((3, P) + page_shape, F32), pltpu.VMEM((3, P) + page_shape, F32),
                        pltpu.SemaphoreType.DMA((3,)), pltpu.SemaphoreType.DMA((3,))])
    return pl.pallas_call(
        functools.partial(_sb_sample_kernel, layer=layer, P=P, n_tok=n_tok),
        grid_spec=grid_spec,
        out_shape=jax.ShapeDtypeStruct((B, n_tok, W), F32),
        compiler_params=_cparams(("arbitrary",)),
        name="sb_sample",
    )(page_table, qbd, kn, vn, _suffix_matrix(PAGE), _suffix_matrix(_NEW_PAD), cache_k, cache_v)


def _df_sample_kernel(pt_ref, lam_ref, q_ref, kn_ref, vn_ref, ck_ref, cv_ref, o_ref,
                      kbuf, vbuf, semk, semv, m_scr, l_scr, acc_scr, *, layer, P, n_tok):
    g = pl.program_id(0)
    total = pl.num_programs(0)
    n_pages = pt_ref.shape[1]
    nch = n_pages // P
    c = g % nch
    slot = g % 2

    page_rows = PAGE * H_DF

    def copies(gg, sl):
        bb = gg // nch
        cc = gg % nch
        cps = []
        for j in range(P):
            page = pt_ref[bb, cc * P + j]
            dst = pl.ds(j * page_rows, page_rows)
            cps.append(pltpu.make_async_copy(ck_ref.at[layer, page], kbuf.at[sl, dst], semk.at[sl]))
            cps.append(pltpu.make_async_copy(cv_ref.at[layer, page], vbuf.at[sl, dst], semv.at[sl]))
        return cps

    @pl.when(g == 0)
    def _():
        for cp in copies(0, 0):
            cp.start()

    @pl.when(g + 1 < total)
    def _():
        for cp in copies(g + 1, 1 - slot):
            cp.start()

    rows = q_ref.shape[1]

    @pl.when(c == 0)
    def _():
        nn = kn_ref.shape[0]
        t_of_row = lax.broadcasted_iota(jnp.int32, (rows, nn), 0) % n_tok
        j_new = lax.broadcasted_iota(jnp.int32, (rows, nn), 1)
        for h in range(H_DF):
            hs = slice(h * DV_DF, (h + 1) * DV_DF)
            s = jnp.where(j_new <= t_of_row, _mm_nt(q_ref[h], kn_ref[:, hs]), -jnp.inf)
            m = jnp.max(s, axis=1, keepdims=True)
            p = jnp.exp2(s - m)
            m_scr[h] = m
            l_scr[h] = jnp.sum(p, axis=1, keepdims=True)
            acc_scr[h] = _mm(p.astype(BF16), vn_ref[:, hs])

    for cp in copies(g, slot):
        cp.wait()
    for h in range(H_DF):
        head_rows = pl.ds(h, P * PAGE, stride=H_DF)
        k = kbuf[slot, head_rows, :].astype(BF16)
        v = vbuf[slot, head_rows, :].astype(BF16)
        m, l, acc = _softmax_step(_mm_nt(q_ref[h], k), v, m_scr[h], l_scr[h], acc_scr[h])
        m_scr[h] = m
        l_scr[h] = l
        acc_scr[h] = acc

    @pl.when(c == nch - 1)
    def _():
        for h in range(H_DF):
            o = acc_scr[h] / l_scr[h]
            o_ref[:, h * DV_DF:(h + 1) * DV_DF] = o[:n_tok] - lam_ref[0, 0] * o[n_tok:]


def _df_sample_call(page_table, lam, q, k_new, v_new, cache_k, cache_v, layer, P=16):
    B, n_tok, W = q.shape
    qh = q.reshape(B, n_tok, H_DF, 2 * DH_DF).transpose(0, 2, 1, 3)
    map_of_col = (jnp.arange(2 * DH_DF) // DH_DF)[None, None, None, None, :]
    map_of_row = jnp.arange(2)[None, None, :, None, None]
    qbd = jnp.where(map_of_col == map_of_row, qh[:, :, None], jnp.zeros((), q.dtype))
    rows = 2 * n_tok
    qbd = qbd.reshape(B, H_DF, rows, 2 * DH_DF)
    kn, vn = _pad_rows(k_new, _NEW_PAD), _pad_rows(v_new, _NEW_PAD)
    n_pages = page_table.shape[1]
    nch = n_pages // P
    page_rows, lanes = cache_k.shape[2:]
    grid_spec = pltpu.PrefetchScalarGridSpec(
        num_scalar_prefetch=1,
        grid=(B * nch,),
        in_specs=[pl.BlockSpec(memory_space=pltpu.SMEM),
                  pl.BlockSpec((None, H_DF, rows, 2 * DH_DF), lambda g, pt: (g // nch, 0, 0, 0)),
                  pl.BlockSpec((None, _NEW_PAD, W), lambda g, pt: (g // nch, 0, 0)),
                  pl.BlockSpec((None, _NEW_PAD, W_DF), lambda g, pt: (g // nch, 0, 0)),
                  pl.BlockSpec(memory_space=pl.ANY),
                  pl.BlockSpec(memory_space=pl.ANY)],
        out_specs=pl.BlockSpec((None, n_tok, W_DF), lambda g, pt: (g // nch, 0, 0)),
        scratch_shapes=[pltpu.VMEM((2, P * page_rows, lanes), F32), pltpu.VMEM((2, P * page_rows, lanes), F32),
                        pltpu.SemaphoreType.DMA((2,)), pltpu.SemaphoreType.DMA((2,)),
                        pltpu.VMEM((H_DF, rows, 1), F32), pltpu.VMEM((H_DF, rows, 1), F32),
                        pltpu.VMEM((H_DF, rows, DV_DF), F32)])
    return pl.pallas_call(
        functools.partial(_df_sample_kernel, layer=layer, P=P, n_tok=n_tok),
        grid_spec=grid_spec,
        out_shape=jax.ShapeDtypeStruct((B, n_tok, W_DF), F32),
        compiler_params=_cparams(("arbitrary",)),
        name="df_sample",
    )(page_table, lam, qbd, kn, vn, cache_k, cache_v)


def _group_mean_matrix(width, d):
    g = np.arange(width) // d
    return jnp.asarray((g[:, None] == g[None, :]).astype(np.float32) / d, BF16)


def _head_norm(o, group_mean, gain):
    return o * lax.rsqrt(_mm_f32_lhs(o * o, group_mean) + EPS) * gain


def _rms_mod(x, g, sc, sh):
    y = x * lax.rsqrt(jnp.mean(x * x, axis=-1, keepdims=True) + EPS) * g
    return y * (1.0 + sc) + sh


def _merge_kernel(x_ref, osb_ref, ohg_ref, ghg_ref, odf_ref, gsb_ref, gnh_ref, gdf_ref, m64_ref, m128_ref,
                  w_ref, g1_ref, n2_ref, sc2_ref, sh2_ref, x_out, h2_out, *, df_scale):
    m64 = m64_ref[...]
    gate = ghg_ref[...]
    o_sb = _head_norm(osb_ref[...], m64, gsb_ref[...]).astype(BF16)
    o_hg = (_head_norm(ohg_ref[...], m64, gnh_ref[...]) * (gate / (1.0 + jnp.exp(-gate)))).astype(BF16)
    o_df = (_head_norm(odf_ref[...], m128_ref[...], gdf_ref[...]) * df_scale).astype(BF16)
    proj = (_mm(o_sb, w_ref[0:W_SB, :]) + _mm(o_hg, w_ref[W_SB:W_SB + W_HG, :])
            + _mm(o_df, w_ref[W_SB + W_HG:, :]))
    x = x_ref[...] + g1_ref[...] * proj
    x_out[...] = x
    h2_out[...] = _rms_mod(x, n2_ref[...], sc2_ref[...], sh2_ref[...]).T.astype(BF16)


def _merge_call(x, o_sb, o_hg, gate, o_df, gn_sb, gn_hg, gn_df, w_out_b, mod, norm2_g, df_scale, tm):
    G, T, D = x.shape
    R = mod.shape[1]
    rb = 1 if R == 1 else tm
    mrow = (lambda comp: (lambda b, i: (b, 0 if R == 1 else i, comp)))
    tok = lambda w: pl.BlockSpec((None, tm, w), lambda b, i: (b, i, 0))
    full = lambda a: pl.BlockSpec(a.shape, lambda b, i: (0,) * a.ndim)
    m64, m128 = _group_mean_matrix(W_SB, DH_SB), _group_mean_matrix(W_DF, DV_DF)
    gsb, ghg, gdf, n2 = gn_sb.reshape(1, -1), gn_hg.reshape(1, -1), gn_df.reshape(1, -1), norm2_g.reshape(1, D)
    return pl.pallas_call(
        functools.partial(_merge_kernel, df_scale=df_scale),
        grid=(G, T // tm),
        in_specs=[tok(D), tok(W_SB), tok(W_HG), tok(W_HG), tok(W_DF), full(gsb), full(ghg), full(gdf),
                  full(m64), full(m128), full(w_out_b),
                  pl.BlockSpec((None, rb, D), mrow(2)), full(n2),
                  pl.BlockSpec((None, rb, D), mrow(4)), pl.BlockSpec((None, rb, D), mrow(3))],
        out_specs=[tok(D), pl.BlockSpec((D, tm), lambda b, i: (0, b * (T // tm) + i))],
        out_shape=[jax.ShapeDtypeStruct((G, T, D), F32), jax.ShapeDtypeStruct((D, G * T), BF16)],
        compiler_params=_cparams(("parallel", "parallel")),
        name="merge_outproj",
    )(x, o_sb, o_hg, gate, o_df, gsb, ghg, gdf, m64, m128, w_out_b, mod, n2, mod, mod)


def _final_norm_kernel(x_ref, g_ref, o_ref):
    x = x_ref[...]
    o_ref[...] = x * lax.rsqrt(jnp.mean(x * x, axis=-1, keepdims=True) + EPS) * g_ref[...]


def _final_norm_call(x, g, tm):
    G, T, D = x.shape
    return pl.pallas_call(
        _final_norm_kernel,
        grid=(G, T // tm),
        in_specs=[pl.BlockSpec((None, tm, D), lambda b, i: (b, i, 0)), pl.BlockSpec((1, D), lambda b, i: (0, 0))],
        out_specs=pl.BlockSpec((None, tm, D), lambda b, i: (b, i, 0)),
        out_shape=jax.ShapeDtypeStruct((G, T, D), F32),
        compiler_params=_cparams(("parallel", "parallel")),
        name="final_norm",
    )(x, g.reshape(1, D))


def _top16(arrays):
    n, L = arrays[0].shape
    iota = lax.broadcasted_iota(jnp.int32, (n, L), 0)
    row16 = lax.broadcasted_iota(jnp.int32, (PEER_TOPK, L), 0)

    def one(j, s, rank, vals):
        m = jnp.max(s, axis=0, keepdims=True)
        idx = jnp.min(jnp.where(s == m, iota, n), axis=0, keepdims=True)
        sel = iota == idx
        rank = jnp.where(sel, lax.convert_element_type(j, F32), rank)
        return jnp.where(sel, -jnp.inf, s), rank, jnp.where(row16 == j, m, vals)

    def body(j, carry):
        return tuple(one(j, *c) for c in carry)

    init = tuple((s, jnp.full((n, L), float(PEER_TOPK), F32), jnp.zeros((PEER_TOPK, L), F32)) for s in arrays)
    out = lax.fori_loop(0, PEER_TOPK, body, init)
    return [(rank, vals) for _, rank, vals in out]


_CAND_ROWS = (16,) + (8,) * 7


def _peer_route_kernel(h2t_ref, wq_ref, k1_ref, k2_ref, rank2_ref, e2_ref, cnt_ref, e1_ref, qt_scr):
    qt_scr[...] = _mm(wq_ref[...], h2t_ref[...])
    L = qt_scr.shape[1]
    dk = k1_ref.shape[2]
    ncand = sum(_CAND_ROWS) + 8
    iota_c = lax.broadcasted_iota(jnp.int32, (ncand, L), 0)
    row8 = lax.broadcasted_iota(jnp.int32, (8, L), 0)

    def head(h, _):
        base = pl.multiple_of(h * 2 * dk, 2 * dk)
        s1 = _mm(k1_ref[h], qt_scr[pl.ds(base, dk), :].astype(BF16))
        s2 = _mm(k2_ref[h], qt_scr[pl.ds(base + dk, dk), :].astype(BF16))
        (rank1, v1), (rank2, v2) = _top16((s1, s2))
        cand = jnp.concatenate(
            [v1[r:r + 1] + v2[0:n] for r, n in enumerate(_CAND_ROWS)] + [v1[8:16] + v2[0:1]], axis=0)
        m0 = v1[0:1] + v2[0:1]

        def pick(j, carry):
            cand, chosen, z = carry
            m = jnp.max(cand, axis=0, keepdims=True)
            idx = jnp.min(jnp.where(cand == m, iota_c, ncand), axis=0, keepdims=True)
            sel = iota_c == idx
            return jnp.where(sel, -jnp.inf, cand), jnp.where(sel, 1.0, chosen), z + jnp.exp(m - m0)

        _, chosen, z = lax.fori_loop(0, PEER_TOPK, pick,
                                     (cand, jnp.zeros((ncand, L), F32), jnp.zeros((1, L), F32)))
        low = jnp.zeros((8, L), F32)
        off = 0
        for r, n in enumerate(_CAND_ROWS):
            low = jnp.where(row8 == r, jnp.sum(chosen[off:off + n], axis=0, keepdims=True), low)
            off += n
        cnt16 = jnp.concatenate([low, chosen[off:off + 8]], axis=0)
        cnt = jnp.zeros_like(s1)
        for r in range(PEER_TOPK):
            cnt = jnp.where(rank1 == float(r), cnt16[r:r + 1], cnt)
        rank2_ref[h] = rank2
        e2_ref[h] = jnp.exp(s2 - v2[0:1])
        cnt_ref[h] = cnt
        e1_ref[h] = jnp.exp(s1 - v1[0:1]) / z
        return 0

    lax.fori_loop(0, k1_ref.shape[0], head, 0)


def _peer_route_call(h2t, wq_t, k1, k2):
    D, N = h2t.shape
    heads, keys, dk = k1.shape
    tn = LANES
    tbl = pl.BlockSpec((heads, keys, tn), lambda i: (0, 0, i))
    shp = jax.ShapeDtypeStruct((heads, keys, N), F32)
    return pl.pallas_call(
        _peer_route_kernel,
        grid=(N // tn,),
        in_specs=[pl.BlockSpec((D, tn), lambda i: (0, i)),
                  pl.BlockSpec(wq_t.shape, lambda i: (0, 0)),
                  pl.BlockSpec(k1.shape, lambda i: (0, 0, 0)),
                  pl.BlockSpec(k2.shape, lambda i: (0, 0, 0))],
        out_specs=[tbl, tbl, tbl, tbl],
        out_shape=[shp, shp, shp, shp],
        scratch_shapes=[pltpu.VMEM((wq_t.shape[0], tn), F32)],
        compiler_params=_cparams(("parallel",)),
        name="peer_route",
    )(h2t, wq_t, k1, k2)


def _peer_dense_kernel(x_ref, g2_ref, h2t_ref, u_ref, vt_ref, rank2_ref, e2_ref, cnt_ref, e1_ref, o_ref,
                       acc_scr, rank2_scr, e2_scr, *, te):
    et = pl.program_id(2)
    heads, keys, _ = rank2_ref.shape

    @pl.when(et == 0)
    def _():
        acc_scr[...] = jnp.zeros_like(acc_scr)
        rank2_scr[...] = rank2_ref[...].astype(BF16)
        e2_scr[...] = e2_ref[...].astype(BF16)

    tn = h2t_ref.shape[1]
    n_i1 = te // keys
    zero = jnp.zeros((), BF16)
    sub = 16

    def pre_activation(lanes):
        return _mm(u_ref[...], h2t_ref[:, lanes])

    def token_group(lanes, a):
        act = (0.5 * a * (1.0 + lax.erf(a * (2.0 ** -0.5)))).astype(BF16)
        width = act.shape[1]

        def row_bf16(ref, h, i1):
            return jnp.broadcast_to(ref[h, pl.ds(i1, 1), lanes], (sub, width)).astype(BF16)

        pieces = []
        for j in range(n_i1):
            i1 = et * n_i1 + j
            cnt_rows = [row_bf16(cnt_ref, h, i1) for h in range(heads)]
            e1_rows = [row_bf16(e1_ref, h, i1) for h in range(heads)]
            for r in range(keys // sub):
                rs = slice(r * sub, (r + 1) * sub)
                g = jnp.zeros((sub, width), BF16)
                for h in range(heads):
                    g = g + jnp.where(rank2_scr[h, rs, lanes] < cnt_rows[h], e1_rows[h] * e2_scr[h, rs, lanes], zero)
                pieces.append(g * act[j * keys + r * sub:j * keys + (r + 1) * sub])
        return _mm(vt_ref[...], jnp.concatenate(pieces, axis=0))

    group = 2 * LANES if tn % (2 * LANES) == 0 else tn
    groups = [slice(t0, t0 + group) for t0 in range(0, tn, group)]
    pre = [pre_activation(lanes) for lanes in groups]
    for lanes, a in zip(groups, pre):
        acc_scr[:, lanes] += token_group(lanes, a)

    @pl.when(et == pl.num_programs(2) - 1)
    def _():
        o_ref[...] = x_ref[...] + g2_ref[...] * acc_scr[...].T


def _peer_dense_call(x, mod, h2t, u_b, vt_b, tables, tn, te):
    G, T, D = x.shape
    E = u_b.shape[0]
    R = mod.shape[1]
    rb = 1 if R == 1 else tn
    nt = T // tn
    heads, keys, _ = tables[0].shape
    tbl = pl.BlockSpec((heads, keys, tn), lambda b, i, e: (0, 0, b * nt + i))
    tok = pl.BlockSpec((None, tn, D), lambda b, i, e: (b, i, 0))
    return pl.pallas_call(
        functools.partial(_peer_dense_kernel, te=te),
        grid=(G, nt, E // te),
        in_specs=[tok, pl.BlockSpec((None, rb, D), lambda b, i, e: (b, 0 if R == 1 else i, 5)),
                  pl.BlockSpec((D, tn), lambda b, i, e: (0, b * nt + i)),
                  pl.BlockSpec((te, D), lambda b, i, e: (e, 0)),
                  pl.BlockSpec((D, te), lambda b, i, e: (0, e)),
                  tbl, tbl, tbl, tbl],
        out_specs=tok,
        out_shape=jax.ShapeDtypeStruct((G, T, D), F32),
        scratch_shapes=[pltpu.VMEM((D, tn), F32),
                        pltpu.VMEM((heads, keys, tn), BF16), pltpu.VMEM((heads, keys, tn), BF16)],
        compiler_params=_cparams(("parallel", "parallel", "arbitrary")),
        name="peer_dense",
    )(x, mod, h2t, u_b, vt_b, *tables)


def _layer(l, x, mod, page_table, caches, s0T, lb_row, lam_total, lam_init, lw, sizes):
    tm, tq_sb, tq_df, chunk, tn_peer, te_peer = sizes
    G, T, D = x.shape
    z = _inproj_call(x, lw["norm1_g"], mod, lw["w_in"], tm)
    if caches is None:
        o_sb = _sb_prompt_call(z["qsb"], z["ksb_b"], z["vsb_b"], tq_sb)
        o_df = _df_prompt_call(lam_total, z["qdf_t"], z["kdf_b"], z["vdf_t"], tq_df)
        tok = lambda a: a
    else:
        B = page_table.shape[0]
        per = lambda a: a.reshape(B, T // B, a.shape[-1])
        tok = lambda a: a.reshape(1, T, a.shape[-1])
        ck_sb, cv_sb, ck_df, cv_df = caches
        o_sb = tok(_sb_sample_call(page_table, per(z["qsb"]), per(z["ksb_b"]), per(z["vsb_b"]), ck_sb, cv_sb, l))
        o_df = tok(_df_sample_call(page_table, lam_total, per(z["qdf"]), per(z["kdf_b"]), per(z["vdf_b"]),
                                   ck_df, cv_df, l))
        z = dict(z, qhg=per(z["qhg"]), ahg=per(z["ahg"]), ihg=per(z["ihg"]))
    o_hg, sT = _hgrn_call(z["qhg"], z["ahg"], z["ihg"], lb_row, s0T, chunk)
    x, h2t = _merge_call(x, o_sb, tok(o_hg), z["ghg"], o_df, lw["gn_sb"], lw["gn_hg"], lw["gn_df"], lw["w_out"],
                         mod, lw["norm2_g"], 1.0 - lam_init, tm)
    tables = _peer_route_call(h2t, lw["w_pq_t"], lw["peer_k1"], lw["peer_k2"])
    x = _peer_dense_call(x, mod, h2t, lw["peer_u"], lw["peer_v_t"], tables, tn_peer, te_peer)
    return x, (z["ksb"], z["vsb"], z["kdf"], z["vdf"], sT)


def kernel(x_prompt, x_sample, cache_sb_k, cache_sb_v, cache_df_k, cache_df_v, state_hgrn, page_table,
           c_prompt, c_sample, w_ada, b_ada, norm1_g, norm2_g, w_in, lb_logits, lam_q1, lam_k1, lam_q2, lam_k2,
           gn_sb, gn_hg, gn_df, w_out, w_pq, peer_k1, peer_k2, peer_u, peer_v, final_g):
    depth = w_ada.shape[0]
    Bp, Tp, D = x_prompt.shape
    Bs, Ts, _ = x_sample.shape
    n_pool = cache_sb_k.shape[1]

    c_all = jnp.concatenate([c_prompt, c_sample], axis=0)
    pad = (-c_all.shape[0]) % 8
    mod_all = _mod_call(jnp.pad(c_all, ((0, pad), (0, 0))), w_ada, b_ada)
    lb_all, lam_diff = _prep_call(lb_logits, lam_q1, lam_k1, lam_q2, lam_k2)

    caches = (jnp.transpose(cache_sb_k, (0, 1, 3, 4, 2)), jnp.transpose(cache_sb_v, (0, 1, 3, 4, 2)),
              cache_df_k.reshape(depth, n_pool, PAGE * H_DF, 2 * DH_DF),
              cache_df_v.reshape(depth, n_pool, PAGE * H_DF, DV_DF))
    s0T_sample = jnp.swapaxes(state_hgrn, 3, 4)
    s0T_prompt = jnp.zeros((Bp, H_HG, DV_HG, DK_HG), F32)

    xp = x_prompt
    xs = x_sample.reshape(1, Bs * Ts, D)
    sizes_p = (512, 128, 256, 128, 512, 1024)
    sizes_s = (Bs * Ts, None, None, Ts, Bs * Ts, 1024)
    outs_p, outs_s = [], []
    for l in range(depth):
        lam_init = 0.8 - 0.6 * math.exp(-0.3 * l)
        lw = dict(norm1_g=norm1_g[l], norm2_g=norm2_g[l], w_in=w_in[l].astype(BF16), gn_sb=gn_sb[l], gn_hg=gn_hg[l],
                  gn_df=gn_df[l], w_out=w_out[l].astype(BF16), w_pq_t=w_pq[l].T.astype(BF16),
                  peer_k1=peer_k1[l].astype(BF16), peer_k2=peer_k2[l].astype(BF16),
                  peer_u=peer_u[l].astype(BF16), peer_v_t=peer_v[l].T.astype(BF16))
        lam_total = (lam_diff[l, 0] + lam_init).reshape(1, 1)
        lb_row = lb_all[l].reshape(1, NK_HG)
        mod_p = mod_all[l, :Bp].reshape(Bp, 1, 6 * D)
        mod_s = jnp.repeat(mod_all[l, Bp:Bp + Bs], Ts, axis=0).reshape(1, Bs * Ts, 6 * D)
        xs, os_ = _layer(l, xs, mod_s, page_table, caches, s0T_sample[l], lb_row, lam_total, lam_init, lw, sizes_s)
        xp, op = _layer(l, xp, mod_p, None, None, s0T_prompt, lb_row, lam_total, lam_init, lw, sizes_p)
        outs_p.append(op)
        outs_s.append(os_)

    y_prompt = _final_norm_call(xp, final_g, 512)
    y_sample = _final_norm_call(xs, final_g, Bs * Ts).reshape(Bs, Ts, D)

    def stack(outs, B, T):
        ksb = jnp.stack([o[0] for o in outs]).reshape(depth, B, T, H_SB, DH_SB)
        vsb = jnp.stack([o[1] for o in outs]).reshape(depth, B, T, H_SB, DH_SB)
        kdf = jnp.stack([o[2] for o in outs]).reshape(depth, B, T, H_DF, 2 * DH_DF)
        vdf = jnp.stack([o[3] for o in outs]).reshape(depth, B, T, H_DF, DV_DF)
        st = jnp.swapaxes(jnp.stack([o[4] for o in outs]), 3, 4)
        return ksb, vsb, kdf, vdf, st

    return (y_prompt, y_sample) + stack(outs_p, Bp, Tp) + stack(outs_s, Bs, Ts)
```

```python
import functools
import math

import numpy as np
import jax
import jax.numpy as jnp
from jax import lax
from jax.experimental import pallas as pl
from jax.experimental.pallas import tpu as pltpu

F32 = jnp.float32
BF16 = jnp.bfloat16
EPS = 1e-6
LANES = 128
VMEM_LIMIT = 48 * 1024 * 1024
SB_DEAD = -120.0

H_SB, DH_SB = 4, 64
H_HG, DK_HG, DV_HG = 4, 128, 64
H_DF, DH_DF, DV_DF = 4, 64, 128
W_SB, W_HG, W_DF = H_SB * DH_SB, H_HG * DV_HG, H_DF * DV_DF
NK_HG = H_HG * DK_HG
NQK_DF = H_DF * 2 * DH_DF
PEER_HEADS, PEER_NKEYS, PEER_TOPK = 8, 128, 16
PAGE = 128


def _cparams(sem):
    return pltpu.CompilerParams(dimension_semantics=sem, vmem_limit_bytes=VMEM_LIMIT)


def _mm(a, b):
    return lax.dot_general(a, b, (((1,), (0,)), ((), ())), preferred_element_type=F32)


def _mm_nt(a, b):
    return lax.dot_general(a, b, (((1,), (1,)), ((), ())), preferred_element_type=F32)


def _mm_tn(a, b):
    return lax.dot_general(a, b, (((0,), (0,)), ((), ())), preferred_element_type=F32)


def _split(x):
    hi = x.astype(BF16)
    lo = (x - hi.astype(F32)).astype(BF16)
    return hi, lo


def _mm_f32_lhs(x, m_bf16):
    hi, lo = _split(x)
    return _mm(hi, m_bf16) + _mm(lo, m_bf16)


def _mm_f32_rhs(m_bf16, x):
    hi, lo = _split(x)
    return _mm(m_bf16, hi) + _mm(m_bf16, lo)


def _log_sigmoid_pair(z):
    l1p = jnp.log1p(jnp.exp(-jnp.abs(z)))
    return jnp.minimum(z, 0.0) - l1p, -(jnp.maximum(z, 0.0) + l1p)


def _mod_kernel(c_ref, w_ref, b_ref, o_ref):
    c = c_ref[...]
    a = c * (1.0 / (1.0 + jnp.exp(-c)))
    ahi, alo = _split(a)
    w = w_ref[...]
    whi, wlo = _split(w)
    o_ref[...] = _mm(ahi, whi) + _mm(ahi, wlo) + _mm(alo, whi) + b_ref[...]


def _mod_call(c_all, w_ada, b_ada):
    depth, d, d6 = w_ada.shape
    r = c_all.shape[0]
    tn = 1024
    return pl.pallas_call(
        _mod_kernel,
        grid=(depth, d6 // tn),
        in_specs=[pl.BlockSpec((r, d), lambda l, j: (0, 0)),
                  pl.BlockSpec((None, d, tn), lambda l, j: (l, 0, j)),
                  pl.BlockSpec((None, 1, tn), lambda l, j: (l, 0, j))],
        out_specs=pl.BlockSpec((None, r, tn), lambda l, j: (l, 0, j)),
        out_shape=jax.ShapeDtypeStruct((depth, r, d6), F32),
        compiler_params=_cparams(("parallel", "parallel")),
        name="adaln_mod",
    )(c_all, w_ada, b_ada.reshape(depth, 1, d6))


def _prep_kernel(lb_ref, q1_ref, k1_ref, q2_ref, k2_ref, lbo_ref, lam_ref):
    lg = lb_ref[...]
    depth = lg.shape[0]
    mx = jnp.max(lg, axis=0, keepdims=True)
    e = jnp.exp(lg - mx)
    p = e / jnp.sum(e, axis=0, keepdims=True)
    run = p[0:1]
    rows = [jnp.zeros_like(run)]
    for l in range(1, depth):
        run = run + p[l:l + 1]
        rows.append(run - p[0:1])
    lbo_ref[...] = jnp.concatenate(rows, axis=0)
    s1 = jnp.sum(q1_ref[...] * k1_ref[...], axis=1, keepdims=True)
    s2 = jnp.sum(q2_ref[...] * k2_ref[...], axis=1, keepdims=True)
    lam_ref[...] = jnp.broadcast_to(jnp.exp(s1) - jnp.exp(s2), lam_ref.shape)


def _prep_call(lb_logits, lq1, lk1, lq2, lk2):
    depth, n = lb_logits.shape
    return pl.pallas_call(
        _prep_kernel,
        out_shape=(jax.ShapeDtypeStruct((depth, n), F32),
                   jax.ShapeDtypeStruct((depth, LANES), F32)),
        name="layer_scalars",
    )(lb_logits, lq1, lk1, lq2, lk2)


_IN_SEGS = (
    ("qsb", W_SB, BF16, DH_SB ** -0.5), ("ksb", W_SB, F32, None), ("vsb", W_SB, F32, None),
    ("qhg", NK_HG, F32, None), ("ahg", NK_HG, F32, None), ("ihg", W_HG, BF16, None), ("ghg", W_HG, F32, None),
    ("qdf", NQK_DF, BF16, DH_DF ** -0.5 * math.log2(math.e)), ("kdf", NQK_DF, F32, None), ("vdf", W_DF, F32, None),
)
_BF16_COPIES = ("ksb", "vsb", "kdf", "vdf")
_T_COPIES = ("qdf", "vdf")
_IN_NAMES = [s[0] for s in _IN_SEGS] + [n + "_b" for n in _BF16_COPIES] + [n + "_t" for n in _T_COPIES]


def _inproj_kernel(x_ref, g_ref, sc_ref, sh_ref, w_ref, *out_refs):
    x = x_ref[...]
    y = x * lax.rsqrt(jnp.mean(x * x, axis=-1, keepdims=True) + EPS) * g_ref[...]
    h = (y * (1.0 + sc_ref[...]) + sh_ref[...]).astype(BF16)
    outs = dict(zip(_IN_NAMES, out_refs))
    off = 0
    for name, width, dt, scale in _IN_SEGS:
        z = _mm(h, w_ref[:, off:off + width])
        off += width
        if scale is not None:
            z = z * scale
        outs[name][...] = z.astype(dt)
        if name in _BF16_COPIES:
            outs[name + "_b"][...] = z.astype(BF16)
        if name in _T_COPIES:
            outs[name + "_t"][...] = z.T.astype(BF16)


def _inproj_call(x, g, mod, w_in_b, tm):
    G, T, D = x.shape
    R = mod.shape[1]
    rb = 1 if R == 1 else tm
    nt = T // tm
    width = w_in_b.shape[1]
    mrow = (lambda comp: (lambda b, i: (b, 0 if R == 1 else i, comp)))
    names = _IN_NAMES
    widths = {s[0]: s[1] for s in _IN_SEGS}
    dts = {s[0]: s[2] for s in _IN_SEGS}
    for n in _BF16_COPIES:
        widths[n + "_b"], dts[n + "_b"] = widths[n], BF16

    def out_spec(n):
        if n.endswith("_t"):
            return pl.BlockSpec((None, widths[n[:-2]], tm), lambda b, i: (b, 0, i))
        return pl.BlockSpec((None, tm, widths[n]), lambda b, i: (b, i, 0))

    def out_shape(n):
        if n.endswith("_t"):
            return jax.ShapeDtypeStruct((G, widths[n[:-2]], T), BF16)
        return jax.ShapeDtypeStruct((G, T, widths[n]), dts[n])

    outs = pl.pallas_call(
        _inproj_kernel,
        grid=(G, nt),
        in_specs=[pl.BlockSpec((None, tm, D), lambda b, i: (b, i, 0)),
                  pl.BlockSpec((1, D), lambda b, i: (0, 0)),
                  pl.BlockSpec((None, rb, D), mrow(1)),
                  pl.BlockSpec((None, rb, D), mrow(0)),
                  pl.BlockSpec((D, width), lambda b, i: (0, 0))],
        out_specs=[out_spec(n) for n in names],
        out_shape=[out_shape(n) for n in names],
        compiler_params=_cparams(("parallel", "parallel")),
        name="norm_inproj",
    )(x, g.reshape(1, D), mod, mod, w_in_b)
    return dict(zip(names, outs))


def _sb_block(q, k, v, c, acc, u, mask, transposed=False):
    z = _mm(q, k) if transposed else _mm_nt(q, k)
    ls_pos, ls_neg = _log_sigmoid_pair(z)
    if mask is not None:
        ls_neg = jnp.where(mask, ls_neg, 0.0)
    surv = _mm_f32_lhs(ls_neg, u) + c
    w = jnp.exp(ls_pos + surv)
    if mask is not None:
        w = jnp.where(mask, w, 0.0)
    w = w.astype(BF16)
    acc = acc + (_mm_nt(w, v) if transposed else _mm(w, v))
    c = c + jnp.sum(ls_neg, axis=1, keepdims=True)
    return c, acc


def _sb_prompt_kernel(q_ref, k_ref, v_ref, u_ref, o_ref, *, tq):
    qi = pl.program_id(1)
    u = u_ref[...]
    row = lax.broadcasted_iota(jnp.int32, (tq, tq), 0)
    col = lax.broadcasted_iota(jnp.int32, (tq, tq), 1)
    diag_mask = col < row
    heads = [slice(h * DH_SB, (h + 1) * DH_SB) for h in range(H_SB)]
    qs = [q_ref[:, hs] for hs in heads]

    def visit(kb, state, mask):
        st = pl.multiple_of(kb * tq, tq)
        return tuple(_sb_block(q, k_ref[pl.ds(st, tq), hs], v_ref[pl.ds(st, tq), hs], c, acc, u, mask)
                     for q, hs, (c, acc) in zip(qs, heads, state))

    init = tuple((jnp.zeros((tq, 1), F32), jnp.zeros((tq, DH_SB), F32)) for _ in heads)
    state = visit(qi, init, diag_mask)

    def cond(carry):
        kb, state = carry
        alive = functools.reduce(jnp.maximum, [c for c, _ in state])
        return jnp.logical_and(kb >= 0, jnp.max(alive) > SB_DEAD)

    def body(carry):
        kb, state = carry
        return kb - 1, visit(kb, state, None)

    _, state = lax.while_loop(cond, body, (qi - 1, state))
    for hs, (_, acc) in zip(heads, state):
        o_ref[:, hs] = acc


def _suffix_matrix(n):
    return jnp.asarray(np.tril(np.ones((n, n), np.float32), -1), BF16)


def _sb_prompt_call(q, k, v, tq):
    B, T, W = q.shape
    return pl.pallas_call(
        functools.partial(_sb_prompt_kernel, tq=tq),
        grid=(B, T // tq),
        in_specs=[pl.BlockSpec((None, tq, W), lambda b, i: (b, i, 0)),
                  pl.BlockSpec((None, T, W), lambda b, i: (b, 0, 0)),
                  pl.BlockSpec((None, T, W), lambda b, i: (b, 0, 0)),
                  pl.BlockSpec((tq, tq), lambda b, i: (0, 0))],
        out_specs=pl.BlockSpec((None, tq, W), lambda b, i: (b, i, 0)),
        out_shape=jax.ShapeDtypeStruct((B, T, W), F32),
        compiler_params=_cparams(("parallel", "parallel")),
        name="sb_prompt",
    )(q, k, v, _suffix_matrix(tq))


def _softmax_step(s, v, m, l, acc):
    m_new = jnp.maximum(m, jnp.max(s, axis=1, keepdims=True))
    alpha = jnp.exp2(m - m_new)
    p = jnp.exp2(s - m_new)
    l = alpha * l + jnp.sum(p, axis=1, keepdims=True)
    acc = alpha * acc + _mm(p.astype(BF16), v)
    return m_new, l, acc


_DF_GROUP = 4


def _df_prompt_kernel(lam_ref, qt_ref, k_ref, vt_ref, o_ref, *, tq):
    qi = pl.program_id(2)
    qt = qt_ref[...]
    first = lax.broadcasted_iota(jnp.int32, qt.shape, 0) < DH_DF
    zero = jnp.zeros((), BF16)
    qbd = jnp.concatenate([jnp.where(first, qt, zero), jnp.where(first, zero, qt)], axis=1)

    def scores(kb):
        return _mm(k_ref[pl.ds(pl.multiple_of(kb * tq, tq), tq), :], qbd)

    def update(kb, s, carry):
        m, l, acc = carry
        m_new = jnp.maximum(m, jnp.max(s, axis=0, keepdims=True))
        alpha = jnp.exp2(m - m_new)
        p = jnp.exp2(s - m_new)
        l = alpha * l + jnp.sum(p, axis=0, keepdims=True)
        acc = alpha * acc + _mm(vt_ref[:, pl.ds(pl.multiple_of(kb * tq, tq), tq)], p.astype(BF16))
        return m_new, l, acc

    key = lax.broadcasted_iota(jnp.int32, (tq, 2 * tq), 0)
    qry = lax.broadcasted_iota(jnp.int32, (tq, 2 * tq), 1) % tq
    init = (jnp.full((1, 2 * tq), -jnp.inf, F32), jnp.zeros((1, 2 * tq), F32), jnp.zeros((DV_DF, 2 * tq), F32))
    carry = update(qi, jnp.where(key <= qry, scores(qi), -jnp.inf), init)

    def group(j, carry):
        blocks = [_DF_GROUP * j + i for i in range(_DF_GROUP)]
        for kb, s in zip(blocks, [scores(kb) for kb in blocks]):
            carry = update(kb, s, carry)
        return carry

    n_groups = qi // _DF_GROUP
    carry = lax.fori_loop(0, n_groups, group, carry)
    _, l, acc = lax.fori_loop(n_groups * _DF_GROUP, qi, lambda kb, c: update(kb, scores(kb), c), carry)
    o = acc / l
    o_ref[...] = (o[:, :tq] - lam_ref[0, 0] * o[:, tq:]).T


def _df_prompt_call(lam, q_t, k, v_t, tq):
    B, T, _ = k.shape
    blk = 2 * DH_DF
    return pl.pallas_call(
        functools.partial(_df_prompt_kernel, tq=tq),
        grid=(B, H_DF, T // tq),
        in_specs=[pl.BlockSpec(memory_space=pltpu.SMEM),
                  pl.BlockSpec((None, blk, tq), lambda b, h, i: (b, h, i)),
                  pl.BlockSpec((None, T, blk), lambda b, h, i: (b, 0, h)),
                  pl.BlockSpec((None, DV_DF, T), lambda b, h, i: (b, h, 0))],
        out_specs=pl.BlockSpec((None, tq, DV_DF), lambda b, h, i: (b, i, h)),
        out_shape=jax.ShapeDtypeStruct((B, T, W_DF), F32),
        compiler_params=_cparams(("parallel", "parallel", "parallel")),
        name="df_prompt",
    )(lam, q_t, k, v_t)


def _hgrn_consts(C):
    levels = int(math.log2(C))
    t = np.arange(C)
    tril = (t[:, None] >= t[None, :]).astype(np.float32)
    lmat, rmat, masks = [], [], []
    for lv in range(levels):
        w = C >> (lv + 1)
        blk = t // (2 * w)
        second = (t % (2 * w)) >= w
        mid = blk * 2 * w + w - 1
        j = t[None, :]
        lmat.append(((j > mid[:, None]) & (j <= t[:, None]) & second[:, None]).astype(np.float32))
        rmat.append(((j > t[:, None]) & (j <= mid[:, None]) & (~second)[:, None]).astype(np.float32))
        masks.append(((blk[:, None] == blk[None, :]) & second[:, None] & (~second)[None, :]).astype(np.float32))
    masks.append(np.eye(C, dtype=np.float32))
    lr = np.concatenate([tril] + lmat + rmat, axis=0)
    return jnp.asarray(lr, BF16), jnp.asarray(np.stack(masks), F32), levels


def _hgrn_kernel(q_ref, a_ref, i_ref, lb_ref, s0_ref, lr_ref, mk_ref, o_ref, sT_ref, st_scr, *, C, levels):
    ci = pl.program_id(1)

    @pl.when(ci == 0)
    def _():
        st_scr[...] = s0_ref[...]

    a = a_ref[...]
    lb = lb_ref[...]
    lsa, _ = _log_sigmoid_pair(a)
    x0 = jnp.log(lb)
    y0 = jnp.log1p(-lb) + lsa
    log_f = jnp.maximum(x0, y0) + jnp.log1p(jnp.exp(-jnp.abs(x0 - y0)))
    kk = (1.0 - lb) / (1.0 + jnp.exp(a))
    sums = _mm_f32_rhs(lr_ref[...], log_f)
    b = sums[:C]
    q = q_ref[...]
    q_in = (q * jnp.exp(b)).astype(BF16)
    k_out = (kk * jnp.exp(b[C - 1:C] - b)).astype(BF16)
    qb = q.astype(BF16)
    kb = kk.astype(BF16)
    dec_l = jnp.exp(b[C - 1:C])
    for h in range(H_HG):
        ks = slice(h * DK_HG, (h + 1) * DK_HG)
        vs = slice(h * DV_HG, (h + 1) * DV_HG)
        att = mk_ref[levels] * _mm_nt(qb[:, ks], kb[:, ks])
        for lv in range(levels):
            dq = sums[(1 + lv) * C:(2 + lv) * C, ks]
            dk = sums[(1 + levels + lv) * C:(2 + levels + lv) * C, ks]
            ql = (q[:, ks] * jnp.exp(dq)).astype(BF16)
            kl = (kk[:, ks] * jnp.exp(dk)).astype(BF16)
            att = att + mk_ref[lv] * _mm_nt(ql, kl)
        iv = i_ref[:, vs]
        sT = st_scr[h]
        o_ref[:, vs] = _mm_nt(q_in[:, ks], sT.astype(BF16)) + _mm(att.astype(BF16), iv)
        st_scr[h] = sT * dec_l[:, ks] + _mm_tn(iv, k_out[:, ks])

    @pl.when(ci == pl.num_programs(1) - 1)
    def _():
        sT_ref[...] = st_scr[...]


def _hgrn_call(q, a, iv, lb_row, s0T, C):
    B, T, _ = q.shape
    lr, masks, levels = _hgrn_consts(C)
    nlr = lr.shape[0]
    return pl.pallas_call(
        functools.partial(_hgrn_kernel, C=C, levels=levels),
        grid=(B, T // C),
        in_specs=[pl.BlockSpec((None, C, NK_HG), lambda b, i: (b, i, 0)),
                  pl.BlockSpec((None, C, NK_HG), lambda b, i: (b, i, 0)),
                  pl.BlockSpec((None, C, W_HG), lambda b, i: (b, i, 0)),
                  pl.BlockSpec((1, NK_HG), lambda b, i: (0, 0)),
                  pl.BlockSpec((None, H_HG, DV_HG, DK_HG), lambda b, i: (b, 0, 0, 0)),
                  pl.BlockSpec((nlr, C), lambda b, i: (0, 0)),
                  pl.BlockSpec((levels + 1, C, C), lambda b, i: (0, 0, 0))],
        out_specs=[pl.BlockSpec((None, C, W_HG), lambda b, i: (b, i, 0)),
                   pl.BlockSpec((None, H_HG, DV_HG, DK_HG), lambda b, i: (b, 0, 0, 0))],
        out_shape=[jax.ShapeDtypeStruct((B, T, W_HG), F32),
                   jax.ShapeDtypeStruct((B, H_HG, DV_HG, DK_HG), F32)],
        scratch_shapes=[pltpu.VMEM((H_HG, DV_HG, DK_HG), F32)],
        compiler_params=_cparams(("parallel", "arbitrary")),
        name="hgrn2",
    )(q, a, iv, lb_row, s0T, lr, masks)


def _page_copies(cache_k, cache_v, kbuf, vbuf, semk, semv, layer, pages, slot):
    cps = []
    for j, page in enumerate(pages):
        cps.append(pltpu.make_async_copy(cache_k.at[layer, page], kbuf.at[slot, j], semk.at[slot]))
        cps.append(pltpu.make_async_copy(cache_v.at[layer, page], vbuf.at[slot, j], semv.at[slot]))
    return cps


def _sb_sample_kernel(pt_ref, q_ref, kn_ref, vn_ref, u_ref, un_ref, ck_ref, cv_ref, o_ref,
                      kbuf, vbuf, semk, semv, *, layer, P, n_tok):
    b = pl.program_id(0)
    nb = pl.num_programs(0)
    n_pages = pt_ref.shape[1]
    nch = n_pages // P
    slot = b % 2

    def copies(bb, ch, sl):
        pages = [pt_ref[bb, n_pages - 1 - ch * P - j] for j in range(P)]
        return _page_copies(ck_ref, cv_ref, kbuf, vbuf, semk, semv, layer, pages, sl)

    @pl.when(b == 0)
    def _():
        for cp in copies(0, 0, 0):
            cp.start()

    @pl.when(b + 1 < nb)
    def _():
        for cp in copies(b + 1, 0, 1 - slot):
            cp.start()

    q = q_ref[...]
    rows = q.shape[0]
    nn = kn_ref.shape[0]
    t_of_row = lax.broadcasted_iota(jnp.int32, (rows, nn), 0) % n_tok
    j_new = lax.broadcasted_iota(jnp.int32, (rows, nn), 1)
    mask_new = jnp.logical_and(j_new < t_of_row, j_new < n_tok)
    c, acc = _sb_block(q, kn_ref[...], vn_ref[...], jnp.zeros((rows, 1), F32),
                       jnp.zeros((rows, W_SB), F32), un_ref[...], mask_new)
    u = u_ref[...]

    def chunk(sl, c, acc):
        for j in range(P):
            kt = kbuf[sl, j].reshape(W_SB, PAGE).astype(BF16)
            vt = vbuf[sl, j].reshape(W_SB, PAGE).astype(BF16)
            c, acc = _sb_block(q, kt, vt, c, acc, u, None, transposed=True)
        return c, acc

    for cp in copies(b, 0, slot):
        cp.wait()
    c, acc = chunk(slot, c, acc)

    def cond(carry):
        ch, c, _ = carry
        return jnp.logical_and(ch < nch, jnp.max(c) > SB_DEAD)

    def body(carry):
        ch, c, acc = carry
        cps = copies(b, ch, 2)
        for cp in cps:
            cp.start()
        for cp in cps:
            cp.wait()
        c, acc = chunk(2, c, acc)
        return ch + 1, c, acc

    _, _, acc = lax.while_loop(cond, body, (jnp.int32(1), c, acc))
    for h in range(H_SB):
        hs = slice(h * DH_SB, (h + 1) * DH_SB)
        o_ref[:, hs] = acc[h * n_tok:(h + 1) * n_tok, hs]


def _block_diag_queries(q, groups):
    B, T, W = q.shape
    d = W // groups
    gid_col = (jnp.arange(W) // d)[None, None, None, :]
    gid_row = jnp.arange(groups)[None, :, None, None]
    out = jnp.where(gid_col == gid_row, q[:, None, :, :], jnp.zeros((), q.dtype))
    return out.reshape(B, groups * T, W)


def _pad_rows(x, n):
    return jnp.pad(x, ((0, 0), (0, n - x.shape[1]), (0, 0)))


_NEW_PAD = 16


def _sb_sample_call(page_table, q, k_new, v_new, cache_k, cache_v, layer, P=4):
    B, n_tok, W = q.shape
    page_shape = cache_k.shape[2:]
    qbd = _block_diag_queries(q, H_SB)
    rows = qbd.shape[1]
    kn, vn = _pad_rows(k_new, _NEW_PAD), _pad_rows(v_new, _NEW_PAD)
    grid_spec = pltpu.PrefetchScalarGridSpec(
        num_scalar_prefetch=1,
        grid=(B,),
        in_specs=[pl.BlockSpec((None, rows, W), lambda b, pt: (b, 0, 0)),
                  pl.BlockSpec((None, _NEW_PAD, W), lambda b, pt: (b, 0, 0)),
                  pl.BlockSpec((None, _NEW_PAD, W), lambda b, pt: (b, 0, 0)),
                  pl.BlockSpec((PAGE, PAGE), lambda b, pt: (0, 0)),
                  pl.BlockSpec((_NEW_PAD, _NEW_PAD), lambda b, pt: (0, 0)),
                  pl.BlockSpec(memory_space=pl.ANY),
                  pl.BlockSpec(memory_space=pl.ANY)],
        out_specs=pl.BlockSpec((None, n_tok, W), lambda b, pt: (b, 0, 0)),
        scratch_shapes=[pltpu.VMEM((3, P) + page_shape, F32), pltpu.VMEM((3, P) + page_shape, F32),
                        pltpu.SemaphoreType.DMA((3,)), pltpu.SemaphoreType.DMA((3,))])
    return pl.pallas_call(
        functools.partial(_sb_sample_kernel, layer=layer, P=P, n_tok=n_tok),
        grid_spec=grid_spec,
        out_shape=jax.ShapeDtypeStruct((B, n_tok, W), F32),
        compiler_params=_cparams(("arbitrary",)),
        name="sb_sample",
    )(page_table, qbd, kn, vn, _suffix_matrix(PAGE), _suffix_matrix(_NEW_PAD), cache_k, cache_v)


def _df_sample_kernel(pt_ref, lam_ref, q_ref, kn_ref, vn_ref, ck_ref, cv_ref, o_ref,
                      kbuf, vbuf, semk, semv, m_scr, l_scr, acc_scr, *, layer, P, n_tok):
    g = pl.program_id(0)
    total = pl.num_programs(0)
    n_pages = pt_ref.shape[1]
    nch = n_pages // P
    c = g % nch
    slot = g % 2

    page_rows = PAGE * H_DF

    def copies(gg, sl):
        bb = gg // nch
        cc = gg % nch
        cps = []
        for j in range(P):
            page = pt_ref[bb, cc * P + j]
            dst = pl.ds(j * page_rows, page_rows)
            cps.append(pltpu.make_async_copy(ck_ref.at[layer, page], kbuf.at[sl, dst], semk.at[sl]))
            cps.append(pltpu.make_async_copy(cv_ref.at[layer, page], vbuf.at[sl, dst], semv.at[sl]))
        return cps

    @pl.when(g == 0)
    def _():
        for cp in copies(0, 0):
            cp.start()

    @pl.when(g + 1 < total)
    def _():
        for cp in copies(g + 1, 1 - slot):
            cp.start()

    rows = q_ref.shape[1]

    @pl.when(c == 0)
    def _():
        nn = kn_ref.shape[0]
        t_of_row = lax.broadcasted_iota(jnp.int32, (rows, nn), 0) % n_tok
        j_new = lax.broadcasted_iota(jnp.int32, (rows, nn), 1)
        for h in range(H_DF):
            hs = slice(h * DV_DF, (h + 1) * DV_DF)
            s = jnp.where(j_new <= t_of_row, _mm_nt(q_ref[h], kn_ref[:, hs]), -jnp.inf)
            m = jnp.max(s, axis=1, keepdims=True)
            p = jnp.exp2(s - m)
            m_scr[h] = m
            l_scr[h] = jnp.sum(p, axis=1, keepdims=True)
            acc_scr[h] = _mm(p.astype(BF16), vn_ref[:, hs])

    for cp in copies(g, slot):
        cp.wait()
    for h in range(H_DF):
        head_rows = pl.ds(h, P * PAGE, stride=H_DF)
        k = kbuf[slot, head_rows, :].astype(BF16)
        v = vbuf[slot, head_rows, :].astype(BF16)
        m, l, acc = _softmax_step(_mm_nt(q_ref[h], k), v, m_scr[h], l_scr[h], acc_scr[h])
        m_scr[h] = m
        l_scr[h] = l
        acc_scr[h] = acc

    @pl.when(c == nch - 1)
    def _():
        for h in range(H_DF):
            o = acc_scr[h] / l_scr[h]
            o_ref[:, h * DV_DF:(h + 1) * DV_DF] = o[:n_tok] - lam_ref[0, 0] * o[n_tok:]


def _df_sample_call(page_table, lam, q, k_new, v_new, cache_k, cache_v, layer, P=16):
    B, n_tok, W = q.shape
    qh = q.reshape(B, n_tok, H_DF, 2 * DH_DF).transpose(0, 2, 1, 3)
    map_of_col = (jnp.arange(2 * DH_DF) // DH_DF)[None, None, None, None, :]
    map_of_row = jnp.arange(2)[None, None, :, None, None]
    qbd = jnp.where(map_of_col == map_of_row, qh[:, :, None], jnp.zeros((), q.dtype))
    rows = 2 * n_tok
    qbd = qbd.reshape(B, H_DF, rows, 2 * DH_DF)
    kn, vn = _pad_rows(k_new, _NEW_PAD), _pad_rows(v_new, _NEW_PAD)
    n_pages = page_table.shape[1]
    nch = n_pages // P
    page_rows, lanes = cache_k.shape[2:]
    grid_spec = pltpu.PrefetchScalarGridSpec(
        num_scalar_prefetch=1,
        grid=(B * nch,),
        in_specs=[pl.BlockSpec(memory_space=pltpu.SMEM),
                  pl.BlockSpec((None, H_DF, rows, 2 * DH_DF), lambda g, pt: (g // nch, 0, 0, 0)),
                  pl.BlockSpec((None, _NEW_PAD, W), lambda g, pt: (g // nch, 0, 0)),
                  pl.BlockSpec((None, _NEW_PAD, W_DF), lambda g, pt: (g // nch, 0, 0)),
                  pl.BlockSpec(memory_space=pl.ANY),
                  pl.BlockSpec(memory_space=pl.ANY)],
        out_specs=pl.BlockSpec((None, n_tok, W_DF), lambda g, pt: (g // nch, 0, 0)),
        scratch_shapes=[pltpu.VMEM((2, P * page_rows, lanes), F32), pltpu.VMEM((2, P * page_rows, lanes), F32),
                        pltpu.SemaphoreType.DMA((2,)), pltpu.SemaphoreType.DMA((2,)),
                        pltpu.VMEM((H_DF, rows, 1), F32), pltpu.VMEM((H_DF, rows, 1), F32),
                        pltpu.VMEM((H_DF, rows, DV_DF), F32)])
    return pl.pallas_call(
        functools.partial(_df_sample_kernel, layer=layer, P=P, n_tok=n_tok),
        grid_spec=grid_spec,
        out_shape=jax.ShapeDtypeStruct((B, n_tok, W_DF), F32),
        compiler_params=_cparams(("arbitrary",)),
        name="df_sample",
    )(page_table, lam, qbd, kn, vn, cache_k, cache_v)


def _group_mean_matrix(width, d):
    g = np.arange(width) // d
    return jnp.asarray((g[:, None] == g[None, :]).astype(np.float32) / d, BF16)


def _head_norm(o, group_mean, gain):
    return o * lax.rsqrt(_mm_f32_lhs(o * o, group_mean) + EPS) * gain


def _rms_mod(x, g, sc, sh):
    y = x * lax.rsqrt(jnp.mean(x * x, axis=-1, keepdims=True) + EPS) * g
    return y * (1.0 + sc) + sh


def _merge_kernel(x_ref, osb_ref, ohg_ref, ghg_ref, odf_ref, gsb_ref, gnh_ref, gdf_ref, m64_ref, m128_ref,
                  w_ref, g1_ref, n2_ref, sc2_ref, sh2_ref, x_out, h2_out, *, df_scale):
    m64 = m64_ref[...]
    gate = ghg_ref[...]
    o_sb = _head_norm(osb_ref[...], m64, gsb_ref[...]).astype(BF16)
    o_hg = (_head_norm(ohg_ref[...], m64, gnh_ref[...]) * (gate / (1.0 + jnp.exp(-gate)))).astype(BF16)
    o_df = (_head_norm(odf_ref[...], m128_ref[...], gdf_ref[...]) * df_scale).astype(BF16)
    proj = (_mm(o_sb, w_ref[0:W_SB, :]) + _mm(o_hg, w_ref[W_SB:W_SB + W_HG, :])
            + _mm(o_df, w_ref[W_SB + W_HG:, :]))
    x = x_ref[...] + g1_ref[...] * proj
    x_out[...] = x
    h2_out[...] = _rms_mod(x, n2_ref[...], sc2_ref[...], sh2_ref[...]).T.astype(BF16)


def _merge_call(x, o_sb, o_hg, gate, o_df, gn_sb, gn_hg, gn_df, w_out_b, mod, norm2_g, df_scale, tm):
    G, T, D = x.shape
    R = mod.shape[1]
    rb = 1 if R == 1 else tm
    mrow = (lambda comp: (lambda b, i: (b, 0 if R == 1 else i, comp)))
    tok = lambda w: pl.BlockSpec((None, tm, w), lambda b, i: (b, i, 0))
    full = lambda a: pl.BlockSpec(a.shape, lambda b, i: (0,) * a.ndim)
    m64, m128 = _group_mean_matrix(W_SB, DH_SB), _group_mean_matrix(W_DF, DV_DF)
    gsb, ghg, gdf, n2 = gn_sb.reshape(1, -1), gn_hg.reshape(1, -1), gn_df.reshape(1, -1), norm2_g.reshape(1, D)
    return pl.pallas_call(
        functools.partial(_merge_kernel, df_scale=df_scale),
        grid=(G, T // tm),
        in_specs=[tok(D), tok(W_SB), tok(W_HG), tok(W_HG), tok(W_DF), full(gsb), full(ghg), full(gdf),
                  full(m64), full(m128), full(w_out_b),
                  pl.BlockSpec((None, rb, D), mrow(2)), full(n2),
                  pl.BlockSpec((None, rb, D), mrow(4)), pl.BlockSpec((None, rb, D), mrow(3))],
        out_specs=[tok(D), pl.BlockSpec((D, tm), lambda b, i: (0, b * (T // tm) + i))],
        out_shape=[jax.ShapeDtypeStruct((G, T, D), F32), jax.ShapeDtypeStruct((D, G * T), BF16)],
        compiler_params=_cparams(("parallel", "parallel")),
        name="merge_outproj",
    )(x, o_sb, o_hg, gate, o_df, gsb, ghg, gdf, m64, m128, w_out_b, mod, n2, mod, mod)


def _final_norm_kernel(x_ref, g_ref, o_ref):
    x = x_ref[...]
    o_ref[...] = x * lax.rsqrt(jnp.mean(x * x, axis=-1, keepdims=True) + EPS) * g_ref[...]


def _final_norm_call(x, g, tm):
    G, T, D = x.shape
    return pl.pallas_call(
        _final_norm_kernel,
        grid=(G, T // tm),
        in_specs=[pl.BlockSpec((None, tm, D), lambda b, i: (b, i, 0)), pl.BlockSpec((1, D), lambda b, i: (0, 0))],
        out_specs=pl.BlockSpec((None, tm, D), lambda b, i: (b, i, 0)),
        out_shape=jax.ShapeDtypeStruct((G, T, D), F32),
        compiler_params=_cparams(("parallel", "parallel")),
        name="final_norm",
    )(x, g.reshape(1, D))


def _pick_max(x, iota, break_ties):
    m = jnp.max(x, axis=0, keepdims=True)
    sel = x == m
    if break_ties:
        sel = iota == jnp.min(jnp.where(sel, iota, x.shape[0]), axis=0, keepdims=True)
    return m, sel


def _top16(arrays, break_ties):
    n, L = arrays[0].shape
    iota = lax.broadcasted_iota(jnp.int32, (n, L), 0)
    row16 = lax.broadcasted_iota(jnp.int32, (PEER_TOPK, L), 0)

    def one(j, s, rank, vals):
        m, sel = _pick_max(s, iota, break_ties)
        rank = jnp.where(sel, lax.convert_element_type(j, F32), rank)
        return jnp.where(sel, -jnp.inf, s), rank, jnp.where(row16 == j, m, vals)

    def body(j, carry):
        return tuple(one(j, *c) for c in carry)

    init = tuple((s, jnp.full((n, L), float(PEER_TOPK), F32), jnp.zeros((PEER_TOPK, L), F32)) for s in arrays)
    out = lax.fori_loop(0, PEER_TOPK, body, init)
    return [(rank, vals) for _, rank, vals in out]


_CAND_ROWS = (16,) + (8,) * 7


def _peer_route_kernel(h2t_ref, wq_ref, k1_ref, k2_ref, rank2_ref, e2_ref, cnt_ref, e1_ref, qt_scr):
    qt_scr[...] = _mm(wq_ref[...], h2t_ref[...])
    L = qt_scr.shape[1]
    dk = k1_ref.shape[2]
    ncand = sum(_CAND_ROWS) + 8
    iota_c = lax.broadcasted_iota(jnp.int32, (ncand, L), 0)
    row8 = lax.broadcasted_iota(jnp.int32, (8, L), 0)

    def head(h, _):
        base = pl.multiple_of(h * 2 * dk, 2 * dk)
        s1 = _mm(k1_ref[h], qt_scr[pl.ds(base, dk), :].astype(BF16))
        s2 = _mm(k2_ref[h], qt_scr[pl.ds(base + dk, dk), :].astype(BF16))
        def select(break_ties):
            (rank1, v1), (rank2, v2) = _top16((s1, s2), break_ties)
            cand = jnp.concatenate(
                [v1[r:r + 1] + v2[0:n] for r, n in enumerate(_CAND_ROWS)] + [v1[8:16] + v2[0:1]], axis=0)
            m0 = v1[0:1] + v2[0:1]

            def pick(j, carry):
                cand, chosen, z = carry
                m, sel = _pick_max(cand, iota_c, break_ties)
                return jnp.where(sel, -jnp.inf, cand), jnp.where(sel, 1.0, chosen), z + jnp.exp(m - m0)

            _, chosen, z = lax.fori_loop(0, PEER_TOPK, pick,
                                         (cand, jnp.zeros((ncand, L), F32), jnp.zeros((1, L), F32)))
            return rank1, rank2, v1[0:1], v2[0:1], chosen, z

        fast = select(False)
        picked = [jnp.sum(jnp.where(r < float(PEER_TOPK), 1.0, 0.0), axis=0, keepdims=True) for r in fast[:2]]
        picked.append(jnp.sum(fast[4], axis=0, keepdims=True))
        excess = functools.reduce(jnp.maximum, picked)
        rank1, rank2, top1, top2, chosen, z = lax.cond(
            jnp.max(excess) > float(PEER_TOPK), lambda: select(True), lambda: fast)
        low = jnp.zeros((8, L), F32)
        off = 0
        for r, n in enumerate(_CAND_ROWS):
            low = jnp.where(row8 == r, jnp.sum(chosen[off:off + n], axis=0, keepdims=True), low)
            off += n
        cnt16 = jnp.concatenate([low, chosen[off:off + 8]], axis=0)
        cnt = jnp.zeros_like(s1)
        for r in range(PEER_TOPK):
            cnt = jnp.where(rank1 == float(r), cnt16[r:r + 1], cnt)
        rank2_ref[h] = rank2
        e2_ref[h] = jnp.exp(s2 - top2)
        cnt_ref[h] = cnt
        e1_ref[h] = jnp.exp(s1 - top1) / z
        return 0

    lax.fori_loop(0, k1_ref.shape[0], head, 0)


def _peer_route_call(h2t, wq_t, k1, k2):
    D, N = h2t.shape
    heads, keys, dk = k1.shape
    tn = LANES
    tbl = pl.BlockSpec((heads, keys, tn), lambda i: (0, 0, i))
    shp = jax.ShapeDtypeStruct((heads, keys, N), F32)
    return pl.pallas_call(
        _peer_route_kernel,
        grid=(N // tn,),
        in_specs=[pl.BlockSpec((D, tn), lambda i: (0, i)),
                  pl.BlockSpec(wq_t.shape, lambda i: (0, 0)),
                  pl.BlockSpec(k1.shape, lambda i: (0, 0, 0)),
                  pl.BlockSpec(k2.shape, lambda i: (0, 0, 0))],
        out_specs=[tbl, tbl, tbl, tbl],
        out_shape=[shp, shp, shp, shp],
        scratch_shapes=[pltpu.VMEM((wq_t.shape[0], tn), F32)],
        compiler_params=_cparams(("parallel",)),
        name="peer_route",
    )(h2t, wq_t, k1, k2)


def _peer_dense_kernel(x_ref, g2_ref, h2t_ref, u_ref, vt_ref, rank2_ref, e2_ref, cnt_ref, e1_ref, o_ref,
                       acc_scr, rank2_scr, e2_scr, *, te):
    et = pl.program_id(2)
    heads, keys, _ = rank2_ref.shape

    @pl.when(et == 0)
    def _():
        acc_scr[...] = jnp.zeros_like(acc_scr)
        rank2_scr[...] = rank2_ref[...].astype(BF16)
        e2_scr[...] = e2_ref[...].astype(BF16)

    tn = h2t_ref.shape[1]
    n_i1 = te // keys
    zero = jnp.zeros((), BF16)
    sub = 16

    def pre_activation(lanes):
        return _mm(u_ref[...], h2t_ref[:, lanes])

    def token_group(lanes, a):
        act = (0.5 * a * (1.0 + lax.erf(a * (2.0 ** -0.5)))).astype(BF16)
        width = act.shape[1]

        def row_bf16(ref, h, i1):
            return jnp.broadcast_to(ref[h, pl.ds(i1, 1), lanes], (sub, width)).astype(BF16)

        pieces = []
        for j in range(n_i1):
            i1 = et * n_i1 + j
            cnt_rows = [row_bf16(cnt_ref, h, i1) for h in range(heads)]
            e1_rows = [row_bf16(e1_ref, h, i1) for h in range(heads)]
            for r in range(keys // sub):
                rs = slice(r * sub, (r + 1) * sub)
                g = jnp.zeros((sub, width), BF16)
                for h in range(heads):
                    g = g + jnp.where(rank2_scr[h, rs, lanes] < cnt_rows[h], e1_rows[h] * e2_scr[h, rs, lanes], zero)
                pieces.append(g * act[j * keys + r * sub:j * keys + (r + 1) * sub])
        return _mm(vt_ref[...], jnp.concatenate(pieces, axis=0))

    group = 2 * LANES if tn % (2 * LANES) == 0 else tn
    groups = [slice(t0, t0 + group) for t0 in range(0, tn, group)]
    pre = [pre_activation(lanes) for lanes in groups]
    for lanes, a in zip(groups, pre):
        acc_scr[:, lanes] += token_group(lanes, a)

    @pl.when(et == pl.num_programs(2) - 1)
    def _():
        o_ref[...] = x_ref[...] + g2_ref[...] * acc_scr[...].T


def _peer_dense_call(x, mod, h2t, u_b, vt_b, tables, tn, te):
    G, T, D = x.shape
    E = u_b.shape[0]
    R = mod.shape[1]
    rb = 1 if R == 1 else tn
    nt = T // tn
    heads, keys, _ = tables[0].shape
    tbl = pl.BlockSpec((heads, keys, tn), lambda b, i, e: (0, 0, b * nt + i))
    tok = pl.BlockSpec((None, tn, D), lambda b, i, e: (b, i, 0))
    return pl.pallas_call(
        functools.partial(_peer_dense_kernel, te=te),
        grid=(G, nt, E // te),
        in_specs=[tok, pl.BlockSpec((None, rb, D), lambda b, i, e: (b, 0 if R == 1 else i, 5)),
                  pl.BlockSpec((D, tn), lambda b, i, e: (0, b * nt + i)),
                  pl.BlockSpec((te, D), lambda b, i, e: (e, 0)),
                  pl.BlockSpec((D, te), lambda b, i, e: (0, e)),
                  tbl, tbl, tbl, tbl],
        out_specs=tok,
        out_shape=jax.ShapeDtypeStruct((G, T, D), F32),
        scratch_shapes=[pltpu.VMEM((D, tn), F32),
                        pltpu.VMEM((heads, keys, tn), BF16), pltpu.VMEM((heads, keys, tn), BF16)],
        compiler_params=_cparams(("parallel", "parallel", "arbitrary")),
        name="peer_dense",
    )(x, mod, h2t, u_b, vt_b, *tables)


def _layer(l, x, mod, page_table, caches, s0T, lb_row, lam_total, lam_init, lw, sizes):
    tm, tq_sb, tq_df, chunk, tn_peer, te_peer = sizes
    G, T, D = x.shape
    z = _inproj_call(x, lw["norm1_g"], mod, lw["w_in"], tm)
    if caches is None:
        o_sb = _sb_prompt_call(z["qsb"], z["ksb_b"], z["vsb_b"], tq_sb)
        o_df = _df_prompt_call(lam_total, z["qdf_t"], z["kdf_b"], z["vdf_t"], tq_df)
        tok = lambda a: a
    else:
        B = page_table.shape[0]
        per = lambda a: a.reshape(B, T // B, a.shape[-1])
        tok = lambda a: a.reshape(1, T, a.shape[-1])
        ck_sb, cv_sb, ck_df, cv_df = caches
        o_sb = tok(_sb_sample_call(page_table, per(z["qsb"]), per(z["ksb_b"]), per(z["vsb_b"]), ck_sb, cv_sb, l))
        o_df = tok(_df_sample_call(page_table, lam_total, per(z["qdf"]), per(z["kdf_b"]), per(z["vdf_b"]),
                                   ck_df, cv_df, l))
        z = dict(z, qhg=per(z["qhg"]), ahg=per(z["ahg"]), ihg=per(z["ihg"]))
    o_hg, sT = _hgrn_call(z["qhg"], z["ahg"], z["ihg"], lb_row, s0T, chunk)
    x, h2t = _merge_call(x, o_sb, tok(o_hg), z["ghg"], o_df, lw["gn_sb"], lw["gn_hg"], lw["gn_df"], lw["w_out"],
                         mod, lw["norm2_g"], 1.0 - lam_init, tm)
    tables = _peer_route_call(h2t, lw["w_pq_t"], lw["peer_k1"], lw["peer_k2"])
    x = _peer_dense_call(x, mod, h2t, lw["peer_u"], lw["peer_v_t"], tables, tn_peer, te_peer)
    return x, (z["ksb"], z["vsb"], z["kdf"], z["vdf"], sT)


def kernel(x_prompt, x_sample, cache_sb_k, cache_sb_v, cache_df_k, cache_df_v, state_hgrn, page_table,
           c_prompt, c_sample, w_ada, b_ada, norm1_g, norm2_g, w_in, lb_logits, lam_q1, lam_k1, lam_q2, lam_k2,
           gn_sb, gn_hg, gn_df, w_out, w_pq, peer_k1, peer_k2, peer_u, peer_v, final_g):
    depth = w_ada.shape[0]
    Bp, Tp, D = x_prompt.shape
    Bs, Ts, _ = x_sample.shape
    n_pool = cache_sb_k.shape[1]

    c_all = jnp.concatenate([c_prompt, c_sample], axis=0)
    pad = (-c_all.shape[0]) % 8
    mod_all = _mod_call(jnp.pad(c_all, ((0, pad), (0, 0))), w_ada, b_ada)
    lb_all, lam_diff = _prep_call(lb_logits, lam_q1, lam_k1, lam_q2, lam_k2)

    caches = (jnp.transpose(cache_sb_k, (0, 1, 3, 4, 2)), jnp.transpose(cache_sb_v, (0, 1, 3, 4, 2)),
              cache_df_k.reshape(depth, n_pool, PAGE * H_DF, 2 * DH_DF),
              cache_df_v.reshape(depth, n_pool, PAGE * H_DF, DV_DF))
    s0T_sample = jnp.swapaxes(state_hgrn, 3, 4)
    s0T_prompt = jnp.zeros((Bp, H_HG, DV_HG, DK_HG), F32)

    xp = x_prompt
    xs = x_sample.reshape(1, Bs * Ts, D)
    sizes_p = (512, 256, 256, 128, 512, 1024)
    sizes_s = (Bs * Ts, None, None, Ts, Bs * Ts, 1024)
    outs_p, outs_s = [], []
    for l in range(depth):
        lam_init = 0.8 - 0.6 * math.exp(-0.3 * l)
        lw = dict(norm1_g=norm1_g[l], norm2_g=norm2_g[l], w_in=w_in[l].astype(BF16), gn_sb=gn_sb[l], gn_hg=gn_hg[l],
                  gn_df=gn_df[l], w_out=w_out[l].astype(BF16), w_pq_t=w_pq[l].T.astype(BF16),
                  peer_k1=peer_k1[l].astype(BF16), peer_k2=peer_k2[l].astype(BF16),
                  peer_u=peer_u[l].astype(BF16), peer_v_t=peer_v[l].T.astype(BF16))
        lam_total = (lam_diff[l, 0] + lam_init).reshape(1, 1)
        lb_row = lb_all[l].reshape(1, NK_HG)
        mod_p = mod_all[l, :Bp].reshape(Bp, 1, 6 * D)
        mod_s = jnp.repeat(mod_all[l, Bp:Bp + Bs], Ts, axis=0).reshape(1, Bs * Ts, 6 * D)
        xs, os_ = _layer(l, xs, mod_s, page_table, caches, s0T_sample[l], lb_row, lam_total, lam_init, lw, sizes_s)
        xp, op = _layer(l, xp, mod_p, None, None, s0T_prompt, lb_row, lam_total, lam_init, lw, sizes_p)
        outs_p.append(op)
        outs_s.append(os_)

    y_prompt = _final_norm_call(xp, final_g, 512)
    y_sample = _final_norm_call(xs, final_g, Bs * Ts).reshape(Bs, Ts, D)

    def stack(outs, B, T):
        ksb = jnp.stack([o[0] for o in outs]).reshape(depth, B, T, H_SB, DH_SB)
        vsb = jnp.stack([o[1] for o in outs]).reshape(depth, B, T, H_SB, DH_SB)
        kdf = jnp.stack([o[2] for o in outs]).reshape(depth, B, T, H_DF, 2 * DH_DF)
        vdf = jnp.stack([o[3] for o in outs]).reshape(depth, B, T, H_DF, DV_DF)
        st = jnp.swapaxes(jnp.stack([o[4] for o in outs]), 3, 4)
        return ksb, vsb, kdf, vdf, st

    return (y_prompt, y_sample) + stack(outs_p, Bp, Tp) + stack(outs_s, Bs, Ts)
```

```python
import functools
import math

import numpy as np
import jax
import jax.numpy as jnp
from jax import lax
from jax.experimental import pallas as pl
from jax.experimental.pallas import tpu as pltpu

F32 = jnp.float32
BF16 = jnp.bfloat16
EPS = 1e-6
LANES = 128
VMEM_LIMIT = 48 * 1024 * 1024
SB_DEAD = -120.0

H_SB, DH_SB = 4, 64
H_HG, DK_HG, DV_HG = 4, 128, 64
H_DF, DH_DF, DV_DF = 4, 64, 128
W_SB, W_HG, W_DF = H_SB * DH_SB, H_HG * DV_HG, H_DF * DV_DF
NK_HG = H_HG * DK_HG
NQK_DF = H_DF * 2 * DH_DF
PEER_HEADS, PEER_NKEYS, PEER_TOPK = 8, 128, 16
PAGE = 128


def _cparams(sem):
    return pltpu.CompilerParams(dimension_semantics=sem, vmem_limit_bytes=VMEM_LIMIT)


def _mm(a, b):
    return lax.dot_general(a, b, (((1,), (0,)), ((), ())), preferred_element_type=F32)


def _mm_nt(a, b):
    return lax.dot_general(a, b, (((1,), (1,)), ((), ())), preferred_element_type=F32)


def _mm_tn(a, b):
    return lax.dot_general(a, b, (((0,), (0,)), ((), ())), preferred_element_type=F32)


def _split(x):
    hi = x.astype(BF16)
    lo = (x - hi.astype(F32)).astype(BF16)
    return hi, lo


def _mm_f32_lhs(x, m_bf16):
    hi, lo = _split(x)
    return _mm(hi, m_bf16) + _mm(lo, m_bf16)


def _mm_f32_rhs(m_bf16, x):
    hi, lo = _split(x)
    return _mm(m_bf16, hi) + _mm(m_bf16, lo)


def _log_sigmoid_pair(z):
    l1p = jnp.log1p(jnp.exp(-jnp.abs(z)))
    return jnp.minimum(z, 0.0) - l1p, -(jnp.maximum(z, 0.0) + l1p)


def _mod_kernel(c_ref, w_ref, b_ref, o_ref):
    c = c_ref[...]
    a = c * (1.0 / (1.0 + jnp.exp(-c)))
    ahi, alo = _split(a)
    w = w_ref[...]
    whi, wlo = _split(w)
    o_ref[...] = _mm(ahi, whi) + _mm(ahi, wlo) + _mm(alo, whi) + b_ref[...]


def _mod_call(c_all, w_ada, b_ada):
    depth, d, d6 = w_ada.shape
    r = c_all.shape[0]
    tn = 1024
    return pl.pallas_call(
        _mod_kernel,
        grid=(depth, d6 // tn),
        in_specs=[pl.BlockSpec((r, d), lambda l, j: (0, 0)),
                  pl.BlockSpec((None, d, tn), lambda l, j: (l, 0, j)),
                  pl.BlockSpec((None, 1, tn), lambda l, j: (l, 0, j))],
        out_specs=pl.BlockSpec((None, r, tn), lambda l, j: (l, 0, j)),
        out_shape=jax.ShapeDtypeStruct((depth, r, d6), F32),
        compiler_params=_cparams(("parallel", "parallel")),
        name="adaln_mod",
    )(c_all, w_ada, b_ada.reshape(depth, 1, d6))


def _prep_kernel(lb_ref, q1_ref, k1_ref, q2_ref, k2_ref, lbo_ref, lam_ref):
    lg = lb_ref[...]
    depth = lg.shape[0]
    mx = jnp.max(lg, axis=0, keepdims=True)
    e = jnp.exp(lg - mx)
    p = e / jnp.sum(e, axis=0, keepdims=True)
    run = p[0:1]
    rows = [jnp.zeros_like(run)]
    for l in range(1, depth):
        run = run + p[l:l + 1]
        rows.append(run - p[0:1])
    lbo_ref[...] = jnp.concatenate(rows, axis=0)
    s1 = jnp.sum(q1_ref[...] * k1_ref[...], axis=1, keepdims=True)
    s2 = jnp.sum(q2_ref[...] * k2_ref[...], axis=1, keepdims=True)
    lam_ref[...] = jnp.broadcast_to(jnp.exp(s1) - jnp.exp(s2), lam_ref.shape)


def _prep_call(lb_logits, lq1, lk1, lq2, lk2):
    depth, n = lb_logits.shape
    return pl.pallas_call(
        _prep_kernel,
        out_shape=(jax.ShapeDtypeStruct((depth, n), F32),
                   jax.ShapeDtypeStruct((depth, LANES), F32)),
        name="layer_scalars",
    )(lb_logits, lq1, lk1, lq2, lk2)


_IN_SEGS = (
    ("qsb", W_SB, BF16, DH_SB ** -0.5), ("ksb", W_SB, F32, None), ("vsb", W_SB, F32, None),
    ("qhg", NK_HG, F32, None), ("ahg", NK_HG, F32, None), ("ihg", W_HG, BF16, None), ("ghg", W_HG, F32, None),
    ("qdf", NQK_DF, BF16, DH_DF ** -0.5 * math.log2(math.e)), ("kdf", NQK_DF, F32, None), ("vdf", W_DF, F32, None),
)
_BF16_COPIES = ("ksb", "vsb", "kdf", "vdf")
_T_COPIES = ("qdf", "vdf")
_IN_NAMES = [s[0] for s in _IN_SEGS] + [n + "_b" for n in _BF16_COPIES] + [n + "_t" for n in _T_COPIES]


def _inproj_kernel(x_ref, g_ref, sc_ref, sh_ref, w_ref, *out_refs):
    x = x_ref[...]
    y = x * lax.rsqrt(jnp.mean(x * x, axis=-1, keepdims=True) + EPS) * g_ref[...]
    h = (y * (1.0 + sc_ref[...]) + sh_ref[...]).astype(BF16)
    outs = dict(zip(_IN_NAMES, out_refs))
    off = 0
    for name, width, dt, scale in _IN_SEGS:
        z = _mm(h, w_ref[:, off:off + width])
        off += width
        if scale is not None:
            z = z * scale
        outs[name][...] = z.astype(dt)
        if name in _BF16_COPIES:
            outs[name + "_b"][...] = z.astype(BF16)
        if name in _T_COPIES:
            outs[name + "_t"][...] = z.T.astype(BF16)


def _inproj_call(x, g, mod, w_in_b, tm):
    G, T, D = x.shape
    R = mod.shape[1]
    rb = 1 if R == 1 else tm
    nt = T // tm
    width = w_in_b.shape[1]
    mrow = (lambda comp: (lambda b, i: (b, 0 if R == 1 else i, comp)))
    names = _IN_NAMES
    widths = {s[0]: s[1] for s in _IN_SEGS}
    dts = {s[0]: s[2] for s in _IN_SEGS}
    for n in _BF16_COPIES:
        widths[n + "_b"], dts[n + "_b"] = widths[n], BF16

    def out_spec(n):
        if n.endswith("_t"):
            return pl.BlockSpec((None, widths[n[:-2]], tm), lambda b, i: (b, 0, i))
        return pl.BlockSpec((None, tm, widths[n]), lambda b, i: (b, i, 0))

    def out_shape(n):
        if n.endswith("_t"):
            return jax.ShapeDtypeStruct((G, widths[n[:-2]], T), BF16)
        return jax.ShapeDtypeStruct((G, T, widths[n]), dts[n])

    outs = pl.pallas_call(
        _inproj_kernel,
        grid=(G, nt),
        in_specs=[pl.BlockSpec((None, tm, D), lambda b, i: (b, i, 0)),
                  pl.BlockSpec((1, D), lambda b, i: (0, 0)),
                  pl.BlockSpec((None, rb, D), mrow(1)),
                  pl.BlockSpec((None, rb, D), mrow(0)),
                  pl.BlockSpec((D, width), lambda b, i: (0, 0))],
        out_specs=[out_spec(n) for n in names],
        out_shape=[out_shape(n) for n in names],
        compiler_params=_cparams(("parallel", "parallel")),
        name="norm_inproj",
    )(x, g.reshape(1, D), mod, mod, w_in_b)
    return dict(zip(names, outs))


def _sb_block(q, k, v, c, acc, u, mask, transposed=False):
    z = _mm(q, k) if transposed else _mm_nt(q, k)
    ls_pos, ls_neg = _log_sigmoid_pair(z)
    if mask is not None:
        ls_neg = jnp.where(mask, ls_neg, 0.0)
    surv = _mm_f32_lhs(ls_neg, u) + c
    w = jnp.exp(ls_pos + surv)
    if mask is not None:
        w = jnp.where(mask, w, 0.0)
    w = w.astype(BF16)
    acc = acc + (_mm_nt(w, v) if transposed else _mm(w, v))
    c = c + jnp.sum(ls_neg, axis=1, keepdims=True)
    return c, acc


def _sb_prompt_kernel(q_ref, k_ref, v_ref, u_ref, o_ref, *, tq):
    qi = pl.program_id(1)
    u = u_ref[...]
    row = lax.broadcasted_iota(jnp.int32, (tq, tq), 0)
    col = lax.broadcasted_iota(jnp.int32, (tq, tq), 1)
    diag_mask = col < row
    heads = [slice(h * DH_SB, (h + 1) * DH_SB) for h in range(H_SB)]
    qs = [q_ref[:, hs] for hs in heads]

    def visit(kb, state, mask):
        st = pl.multiple_of(kb * tq, tq)
        return tuple(_sb_block(q, k_ref[pl.ds(st, tq), hs], v_ref[pl.ds(st, tq), hs], c, acc, u, mask)
                     for q, hs, (c, acc) in zip(qs, heads, state))

    init = tuple((jnp.zeros((tq, 1), F32), jnp.zeros((tq, DH_SB), F32)) for _ in heads)
    state = visit(qi, init, diag_mask)

    def cond(carry):
        kb, state = carry
        alive = functools.reduce(jnp.maximum, [c for c, _ in state])
        return jnp.logical_and(kb >= 0, jnp.max(alive) > SB_DEAD)

    def body(carry):
        kb, state = carry
        return kb - 1, visit(kb, state, None)

    _, state = lax.while_loop(cond, body, (qi - 1, state))
    for hs, (_, acc) in zip(heads, state):
        o_ref[:, hs] = acc


def _suffix_matrix(n):
    return jnp.asarray(np.tril(np.ones((n, n), np.float32), -1), BF16)


def _sb_prompt_call(q, k, v, tq):
    B, T, W = q.shape
    return pl.pallas_call(
        functools.partial(_sb_prompt_kernel, tq=tq),
        grid=(B, T // tq),
        in_specs=[pl.BlockSpec((None, tq, W), lambda b, i: (b, i, 0)),
                  pl.BlockSpec((None, T, W), lambda b, i: (b, 0, 0)),
                  pl.BlockSpec((None, T, W), lambda b, i: (b, 0, 0)),
                  pl.BlockSpec((tq, tq), lambda b, i: (0, 0))],
        out_specs=pl.BlockSpec((None, tq, W), lambda b, i: (b, i, 0)),
        out_shape=jax.ShapeDtypeStruct((B, T, W), F32),
        compiler_params=_cparams(("parallel", "parallel")),
        name="sb_prompt",
    )(q, k, v, _suffix_matrix(tq))


def _softmax_step(s, v, m, l, acc):
    m_new = jnp.maximum(m, jnp.max(s, axis=1, keepdims=True))
    alpha = jnp.exp2(m - m_new)
    p = jnp.exp2(s - m_new)
    l = alpha * l + jnp.sum(p, axis=1, keepdims=True)
    acc = alpha * acc + _mm(p.astype(BF16), v)
    return m_new, l, acc


_DF_GROUP = 4


def _df_prompt_kernel(lam_ref, qt_ref, k_ref, vt_ref, o_ref, *, tq):
    qi = pl.program_id(2)
    qt = qt_ref[...]
    first = lax.broadcasted_iota(jnp.int32, qt.shape, 0) < DH_DF
    zero = jnp.zeros((), BF16)
    qbd = jnp.concatenate([jnp.where(first, qt, zero), jnp.where(first, zero, qt)], axis=1)

    def scores(kb):
        return _mm(k_ref[pl.ds(pl.multiple_of(kb * tq, tq), tq), :], qbd)

    def update(kb, s, carry):
        m, l, acc = carry
        m_new = jnp.maximum(m, jnp.max(s, axis=0, keepdims=True))
        alpha = jnp.exp2(m - m_new)
        p = jnp.exp2(s - m_new)
        l = alpha * l + jnp.sum(p, axis=0, keepdims=True)
        acc = alpha * acc + _mm(vt_ref[:, pl.ds(pl.multiple_of(kb * tq, tq), tq)], p.astype(BF16))
        return m_new, l, acc

    key = lax.broadcasted_iota(jnp.int32, (tq, 2 * tq), 0)
    qry = lax.broadcasted_iota(jnp.int32, (tq, 2 * tq), 1) % tq
    init = (jnp.full((1, 2 * tq), -jnp.inf, F32), jnp.zeros((1, 2 * tq), F32), jnp.zeros((DV_DF, 2 * tq), F32))
    carry = update(qi, jnp.where(key <= qry, scores(qi), -jnp.inf), init)

    def group(j, carry):
        blocks = [_DF_GROUP * j + i for i in range(_DF_GROUP)]
        for kb, s in zip(blocks, [scores(kb) for kb in blocks]):
            carry = update(kb, s, carry)
        return carry

    n_groups = qi // _DF_GROUP
    carry = lax.fori_loop(0, n_groups, group, carry)
    _, l, acc = lax.fori_loop(n_groups * _DF_GROUP, qi, lambda kb, c: update(kb, scores(kb), c), carry)
    o = acc / l
    o_ref[...] = (o[:, :tq] - lam_ref[0, 0] * o[:, tq:]).T


def _df_prompt_call(lam, q_t, k, v_t, tq):
    B, T, _ = k.shape
    blk = 2 * DH_DF
    return pl.pallas_call(
        functools.partial(_df_prompt_kernel, tq=tq),
        grid=(B, H_DF, T // tq),
        in_specs=[pl.BlockSpec(memory_space=pltpu.SMEM),
                  pl.BlockSpec((None, blk, tq), lambda b, h, i: (b, h, i)),
                  pl.BlockSpec((None, T, blk), lambda b, h, i: (b, 0, h)),
                  pl.BlockSpec((None, DV_DF, T), lambda b, h, i: (b, h, 0))],
        out_specs=pl.BlockSpec((None, tq, DV_DF), lambda b, h, i: (b, i, h)),
        out_shape=jax.ShapeDtypeStruct((B, T, W_DF), F32),
        compiler_params=_cparams(("parallel", "parallel", "parallel")),
        name="df_prompt",
    )(lam, q_t, k, v_t)


def _hgrn_consts(C):
    levels = int(math.log2(C))
    t = np.arange(C)
    tril = (t[:, None] >= t[None, :]).astype(np.float32)
    lmat, rmat, masks = [], [], []
    for lv in range(levels):
        w = C >> (lv + 1)
        blk = t // (2 * w)
        second = (t % (2 * w)) >= w
        mid = blk * 2 * w + w - 1
        j = t[None, :]
        lmat.append(((j > mid[:, None]) & (j <= t[:, None]) & second[:, None]).astype(np.float32))
        rmat.append(((j > t[:, None]) & (j <= mid[:, None]) & (~second)[:, None]).astype(np.float32))
        masks.append(((blk[:, None] == blk[None, :]) & second[:, None] & (~second)[None, :]).astype(np.float32))
    masks.append(np.eye(C, dtype=np.float32))
    lr = np.concatenate([tril] + lmat + rmat, axis=0)
    return jnp.asarray(lr, BF16), jnp.asarray(np.stack(masks), F32), levels


def _hgrn_kernel(q_ref, a_ref, i_ref, lb_ref, s0_ref, lr_ref, mk_ref, o_ref, sT_ref, st_scr, *, C, levels):
    ci = pl.program_id(1)

    @pl.when(ci == 0)
    def _():
        st_scr[...] = s0_ref[...]

    a = a_ref[...]
    lb = lb_ref[...]
    lsa, _ = _log_sigmoid_pair(a)
    x0 = jnp.log(lb)
    y0 = jnp.log1p(-lb) + lsa
    log_f = jnp.maximum(x0, y0) + jnp.log1p(jnp.exp(-jnp.abs(x0 - y0)))
    kk = (1.0 - lb) / (1.0 + jnp.exp(a))
    sums = _mm_f32_rhs(lr_ref[...], log_f)
    b = sums[:C]
    q = q_ref[...]
    q_in = (q * jnp.exp(b)).astype(BF16)
    k_out = (kk * jnp.exp(b[C - 1:C] - b)).astype(BF16)
    qb = q.astype(BF16)
    kb = kk.astype(BF16)
    dec_l = jnp.exp(b[C - 1:C])
    for h in range(H_HG):
        ks = slice(h * DK_HG, (h + 1) * DK_HG)
        vs = slice(h * DV_HG, (h + 1) * DV_HG)
        att = mk_ref[levels] * _mm_nt(qb[:, ks], kb[:, ks])
        for lv in range(levels):
            dq = sums[(1 + lv) * C:(2 + lv) * C, ks]
            dk = sums[(1 + levels + lv) * C:(2 + levels + lv) * C, ks]
            ql = (q[:, ks] * jnp.exp(dq)).astype(BF16)
            kl = (kk[:, ks] * jnp.exp(dk)).astype(BF16)
            att = att + mk_ref[lv] * _mm_nt(ql, kl)
        iv = i_ref[:, vs]
        sT = st_scr[h]
        o_ref[:, vs] = _mm_nt(q_in[:, ks], sT.astype(BF16)) + _mm(att.astype(BF16), iv)
        st_scr[h] = sT * dec_l[:, ks] + _mm_tn(iv, k_out[:, ks])

    @pl.when(ci == pl.num_programs(1) - 1)
    def _():
        sT_ref[...] = st_scr[...]


def _hgrn_call(q, a, iv, lb_row, s0T, C):
    B, T, _ = q.shape
    lr, masks, levels = _hgrn_consts(C)
    nlr = lr.shape[0]
    return pl.pallas_call(
        functools.partial(_hgrn_kernel, C=C, levels=levels),
        grid=(B, T // C),
        in_specs=[pl.BlockSpec((None, C, NK_HG), lambda b, i: (b, i, 0)),
                  pl.BlockSpec((None, C, NK_HG), lambda b, i: (b, i, 0)),
                  pl.BlockSpec((None, C, W_HG), lambda b, i: (b, i, 0)),
                  pl.BlockSpec((1, NK_HG), lambda b, i: (0, 0)),
                  pl.BlockSpec((None, H_HG, DV_HG, DK_HG), lambda b, i: (b, 0, 0, 0)),
                  pl.BlockSpec((nlr, C), lambda b, i: (0, 0)),
                  pl.BlockSpec((levels + 1, C, C), lambda b, i: (0, 0, 0))],
        out_specs=[pl.BlockSpec((None, C, W_HG), lambda b, i: (b, i, 0)),
                   pl.BlockSpec((None, H_HG, DV_HG, DK_HG), lambda b, i: (b, 0, 0, 0))],
        out_shape=[jax.ShapeDtypeStruct((B, T, W_HG), F32),
                   jax.ShapeDtypeStruct((B, H_HG, DV_HG, DK_HG), F32)],
        scratch_shapes=[pltpu.VMEM((H_HG, DV_HG, DK_HG), F32)],
        compiler_params=_cparams(("parallel", "arbitrary")),
        name="hgrn2",
    )(q, a, iv, lb_row, s0T, lr, masks)


def _page_copies(cache_k, cache_v, kbuf, vbuf, semk, semv, layer, pages, slot):
    cps = []
    for j, page in enumerate(pages):
        cps.append(pltpu.make_async_copy(cache_k.at[layer, page], kbuf.at[slot, j], semk.at[slot]))
        cps.append(pltpu.make_async_copy(cache_v.at[layer, page], vbuf.at[slot, j], semv.at[slot]))
    return cps


def _sb_sample_kernel(pt_ref, q_ref, kn_ref, vn_ref, u_ref, un_ref, ck_ref, cv_ref, o_ref,
                      kbuf, vbuf, semk, semv, *, layer, P, n_tok):
    b = pl.program_id(0)
    nb = pl.num_programs(0)
    n_pages = pt_ref.shape[1]
    nch = n_pages // P
    slot = b % 2

    def copies(bb, ch, sl):
        pages = [pt_ref[bb, n_pages - 1 - ch * P - j] for j in range(P)]
        return _page_copies(ck_ref, cv_ref, kbuf, vbuf, semk, semv, layer, pages, sl)

    @pl.when(b == 0)
    def _():
        for cp in copies(0, 0, 0):
            cp.start()

    @pl.when(b + 1 < nb)
    def _():
        for cp in copies(b + 1, 0, 1 - slot):
            cp.start()

    q = q_ref[...]
    rows = q.shape[0]
    nn = kn_ref.shape[0]
    t_of_row = lax.broadcasted_iota(jnp.int32, (rows, nn), 0) % n_tok
    j_new = lax.broadcasted_iota(jnp.int32, (rows, nn), 1)
    mask_new = jnp.logical_and(j_new < t_of_row, j_new < n_tok)
    c, acc = _sb_block(q, kn_ref[...], vn_ref[...], jnp.zeros((rows, 1), F32),
                       jnp.zeros((rows, W_SB), F32), un_ref[...], mask_new)
    u = u_ref[...]

    def chunk(sl, c, acc):
        for j in range(P):
            kt = kbuf[sl, j].reshape(W_SB, PAGE).astype(BF16)
            vt = vbuf[sl, j].reshape(W_SB, PAGE).astype(BF16)
            c, acc = _sb_block(q, kt, vt, c, acc, u, None, transposed=True)
        return c, acc

    for cp in copies(b, 0, slot):
        cp.wait()
    c, acc = chunk(slot, c, acc)

    def cond(carry):
        ch, c, _ = carry
        return jnp.logical_and(ch < nch, jnp.max(c) > SB_DEAD)

    def body(carry):
        ch, c, acc = carry
        cps = copies(b, ch, 2)
        for cp in cps:
            cp.start()
        for cp in cps:
            cp.wait()
        c, acc = chunk(2, c, acc)
        return ch + 1, c, acc

    _, _, acc = lax.while_loop(cond, body, (jnp.int32(1), c, acc))
    for h in range(H_SB):
        hs = slice(h * DH_SB, (h + 1) * DH_SB)
        o_ref[:, hs] = acc[h * n_tok:(h + 1) * n_tok, hs]


def _block_diag_queries(q, groups):
    B, T, W = q.shape
    d = W // groups
    gid_col = (jnp.arange(W) // d)[None, None, None, :]
    gid_row = jnp.arange(groups)[None, :, None, None]
    out = jnp.where(gid_col == gid_row, q[:, None, :, :], jnp.zeros((), q.dtype))
    return out.reshape(B, groups * T, W)


def _pad_rows(x, n):
    return jnp.pad(x, ((0, 0), (0, n - x.shape[1]), (0, 0)))


_NEW_PAD = 16


def _sb_sample_call(page_table, q, k_new, v_new, cache_k, cache_v, layer, P=4):
    B, n_tok, W = q.shape
    page_shape = cache_k.shape[2:]
    qbd = _block_diag_queries(q, H_SB)
    rows = qbd.shape[1]
    kn, vn = _pad_rows(k_new, _NEW_PAD), _pad_rows(v_new, _NEW_PAD)
    grid_spec = pltpu.PrefetchScalarGridSpec(
        num_scalar_prefetch=1,
        grid=(B,),
        in_specs=[pl.BlockSpec((None, rows, W), lambda b, pt: (b, 0, 0)),
                  pl.BlockSpec((None, _NEW_PAD, W), lambda b, pt: (b, 0, 0)),
                  pl.BlockSpec((None, _NEW_PAD, W), lambda b, pt: (b, 0, 0)),
                  pl.BlockSpec((PAGE, PAGE), lambda b, pt: (0, 0)),
                  pl.BlockSpec((_NEW_PAD, _NEW_PAD), lambda b, pt: (0, 0)),
                  pl.BlockSpec(memory_space=pl.ANY),
                  pl.BlockSpec(memory_space=pl.ANY)],
        out_specs=pl.BlockSpec((None, n_tok, W), lambda b, pt: (b, 0, 0)),
        scratch_shapes=[pltpu.VMEM((3, P) + page_shape, F32), pltpu.VMEM((3, P) + page_shape, F32),
                        pltpu.SemaphoreType.DMA((3,)), pltpu.SemaphoreType.DMA((3,))])
    return pl.pallas_call(
        functools.partial(_sb_sample_kernel, layer=layer, P=P, n_tok=n_tok),
        grid_spec=grid_spec,
        out_shape=jax.ShapeDtypeStruct((B, n_tok, W), F32),
        compiler_params=_cparams(("arbitrary",)),
        name="sb_sample",
    )(page_table, qbd, kn, vn, _suffix_matrix(PAGE), _suffix_matrix(_NEW_PAD), cache_k, cache_v)


def _df_sample_kernel(pt_ref, lam_ref, q_ref, kn_ref, vn_ref, ck_ref, cv_ref, o_ref,
                      kbuf, vbuf, semk, semv, m_scr, l_scr, acc_scr, *, layer, P, n_tok):
    g = pl.program_id(0)
    total = pl.num_programs(0)
    n_pages = pt_ref.shape[1]
    nch = n_pages // P
    c = g % nch
    slot = g % 2

    page_rows = PAGE * H_DF

    def copies(gg, sl):
        bb = gg // nch
        cc = gg % nch
        cps = []
        for j in range(P):
            page = pt_ref[bb, cc * P + j]
            dst = pl.ds(j * page_rows, page_rows)
            cps.append(pltpu.make_async_copy(ck_ref.at[layer, page], kbuf.at[sl, dst], semk.at[sl]))
            cps.append(pltpu.make_async_copy(cv_ref.at[layer, page], vbuf.at[sl, dst], semv.at[sl]))
        return cps

    @pl.when(g == 0)
    def _():
        for cp in copies(0, 0):
            cp.start()

    @pl.when(g + 1 < total)
    def _():
        for cp in copies(g + 1, 1 - slot):
            cp.start()

    rows = q_ref.shape[1]

    @pl.when(c == 0)
    def _():
        nn = kn_ref.shape[0]
        t_of_row = lax.broadcasted_iota(jnp.int32, (rows, nn), 0) % n_tok
        j_new = lax.broadcasted_iota(jnp.int32, (rows, nn), 1)
        for h in range(H_DF):
            hs = slice(h * DV_DF, (h + 1) * DV_DF)
            s = jnp.where(j_new <= t_of_row, _mm_nt(q_ref[h], kn_ref[:, hs]), -jnp.inf)
            m = jnp.max(s, axis=1, keepdims=True)
            p = jnp.exp2(s - m)
            m_scr[h] = m
            l_scr[h] = jnp.sum(p, axis=1, keepdims=True)
            acc_scr[h] = _mm(p.astype(BF16), vn_ref[:, hs])

    for cp in copies(g, slot):
        cp.wait()
    for h in range(H_DF):
        head_rows = pl.ds(h, P * PAGE, stride=H_DF)
        k = kbuf[slot, head_rows, :].astype(BF16)
        v = vbuf[slot, head_rows, :].astype(BF16)
        m, l, acc = _softmax_step(_mm_nt(q_ref[h], k), v, m_scr[h], l_scr[h], acc_scr[h])
        m_scr[h] = m
        l_scr[h] = l
        acc_scr[h] = acc

    @pl.when(c == nch - 1)
    def _():
        for h in range(H_DF):
            o = acc_scr[h] / l_scr[h]
            o_ref[:, h * DV_DF:(h + 1) * DV_DF] = o[:n_tok] - lam_ref[0, 0] * o[n_tok:]


def _df_sample_call(page_table, lam, q, k_new, v_new, cache_k, cache_v, layer, P=16):
    B, n_tok, W = q.shape
    qh = q.reshape(B, n_tok, H_DF, 2 * DH_DF).transpose(0, 2, 1, 3)
    map_of_col = (jnp.arange(2 * DH_DF) // DH_DF)[None, None, None, None, :]
    map_of_row = jnp.arange(2)[None, None, :, None, None]
    qbd = jnp.where(map_of_col == map_of_row, qh[:, :, None], jnp.zeros((), q.dtype))
    rows = 2 * n_tok
    qbd = qbd.reshape(B, H_DF, rows, 2 * DH_DF)
    kn, vn = _pad_rows(k_new, _NEW_PAD), _pad_rows(v_new, _NEW_PAD)
    n_pages = page_table.shape[1]
    nch = n_pages // P
    page_rows, lanes = cache_k.shape[2:]
    grid_spec = pltpu.PrefetchScalarGridSpec(
        num_scalar_prefetch=1,
        grid=(B * nch,),
        in_specs=[pl.BlockSpec(memory_space=pltpu.SMEM),
                  pl.BlockSpec((None, H_DF, rows, 2 * DH_DF), lambda g, pt: (g // nch, 0, 0, 0)),
                  pl.BlockSpec((None, _NEW_PAD, W), lambda g, pt: (g // nch, 0, 0)),
                  pl.BlockSpec((None, _NEW_PAD, W_DF), lambda g, pt: (g // nch, 0, 0)),
                  pl.BlockSpec(memory_space=pl.ANY),
                  pl.BlockSpec(memory_space=pl.ANY)],
        out_specs=pl.BlockSpec((None, n_tok, W_DF), lambda g, pt: (g // nch, 0, 0)),
        scratch_shapes=[pltpu.VMEM((2, P * page_rows, lanes), F32), pltpu.VMEM((2, P * page_rows, lanes), F32),
                        pltpu.SemaphoreType.DMA((2,)), pltpu.SemaphoreType.DMA((2,)),
                        pltpu.VMEM((H_DF, rows, 1), F32), pltpu.VMEM((H_DF, rows, 1), F32),
                        pltpu.VMEM((H_DF, rows, DV_DF), F32)])
    return pl.pallas_call(
        functools.partial(_df_sample_kernel, layer=layer, P=P, n_tok=n_tok),
        grid_spec=grid_spec,
        out_shape=jax.ShapeDtypeStruct((B, n_tok, W_DF), F32),
        compiler_params=_cparams(("arbitrary",)),
        name="df_sample",
    )(page_table, lam, qbd, kn, vn, cache_k, cache_v)


def _group_mean_matrix(width, d):
    g = np.arange(width) // d
    return jnp.asarray((g[:, None] == g[None, :]).astype(np.float32) / d, BF16)


def _head_norm(o, group_mean, gain):
    return o * lax.rsqrt(_mm_f32_lhs(o * o, group_mean) + EPS) * gain


def _rms_mod(x, g, sc, sh):
    y = x * lax.rsqrt(jnp.mean(x * x, axis=-1, keepdims=True) + EPS) * g
    return y * (1.0 + sc) + sh


def _merge_kernel(x_ref, osb_ref, ohg_ref, ghg_ref, odf_ref, gsb_ref, gnh_ref, gdf_ref, m64_ref, m128_ref,
                  w_ref, g1_ref, n2_ref, sc2_ref, sh2_ref, x_out, h2_out, *, df_scale):
    m64 = m64_ref[...]
    gate = ghg_ref[...]
    o_sb = _head_norm(osb_ref[...], m64, gsb_ref[...]).astype(BF16)
    o_hg = (_head_norm(ohg_ref[...], m64, gnh_ref[...]) * (gate / (1.0 + jnp.exp(-gate)))).astype(BF16)
    o_df = (_head_norm(odf_ref[...], m128_ref[...], gdf_ref[...]) * df_scale).astype(BF16)
    proj = (_mm(o_sb, w_ref[0:W_SB, :]) + _mm(o_hg, w_ref[W_SB:W_SB + W_HG, :])
            + _mm(o_df, w_ref[W_SB + W_HG:, :]))
    x = x_ref[...] + g1_ref[...] * proj
    x_out[...] = x
    h2_out[...] = _rms_mod(x, n2_ref[...], sc2_ref[...], sh2_ref[...]).T.astype(BF16)


def _merge_call(x, o_sb, o_hg, gate, o_df, gn_sb, gn_hg, gn_df, w_out_b, mod, norm2_g, df_scale, tm):
    G, T, D = x.shape
    R = mod.shape[1]
    rb = 1 if R == 1 else tm
    mrow = (lambda comp: (lambda b, i: (b, 0 if R == 1 else i, comp)))
    tok = lambda w: pl.BlockSpec((None, tm, w), lambda b, i: (b, i, 0))
    full = lambda a: pl.BlockSpec(a.shape, lambda b, i: (0,) * a.ndim)
    m64, m128 = _group_mean_matrix(W_SB, DH_SB), _group_mean_matrix(W_DF, DV_DF)
    gsb, ghg, gdf, n2 = gn_sb.reshape(1, -1), gn_hg.reshape(1, -1), gn_df.reshape(1, -1), norm2_g.reshape(1, D)
    return pl.pallas_call(
        functools.partial(_merge_kernel, df_scale=df_scale),
        grid=(G, T // tm),
        in_specs=[tok(D), tok(W_SB), tok(W_HG), tok(W_HG), tok(W_DF), full(gsb), full(ghg), full(gdf),
                  full(m64), full(m128), full(w_out_b),
                  pl.BlockSpec((None, rb, D), mrow(2)), full(n2),
                  pl.BlockSpec((None, rb, D), mrow(4)), pl.BlockSpec((None, rb, D), mrow(3))],
        out_specs=[tok(D), pl.BlockSpec((D, tm), lambda b, i: (0, b * (T // tm) + i))],
        out_shape=[jax.ShapeDtypeStruct((G, T, D), F32), jax.ShapeDtypeStruct((D, G * T), BF16)],
        compiler_params=_cparams(("parallel", "parallel")),
        name="merge_outproj",
    )(x, o_sb, o_hg, gate, o_df, gsb, ghg, gdf, m64, m128, w_out_b, mod, n2, mod, mod)


def _final_norm_kernel(x_ref, g_ref, o_ref):
    x = x_ref[...]
    o_ref[...] = x * lax.rsqrt(jnp.mean(x * x, axis=-1, keepdims=True) + EPS) * g_ref[...]


def _final_norm_call(x, g, tm):
    G, T, D = x.shape
    return pl.pallas_call(
        _final_norm_kernel,
        grid=(G, T // tm),
        in_specs=[pl.BlockSpec((None, tm, D), lambda b, i: (b, i, 0)), pl.BlockSpec((1, D), lambda b, i: (0, 0))],
        out_specs=pl.BlockSpec((None, tm, D), lambda b, i: (b, i, 0)),
        out_shape=jax.ShapeDtypeStruct((G, T, D), F32),
        compiler_params=_cparams(("parallel", "parallel")),
        name="final_norm",
    )(x, g.reshape(1, D))


def _pick_max(x, iota, break_ties):
    m = jnp.max(x, axis=0, keepdims=True)
    sel = x == m
    if break_ties:
        sel = iota == jnp.min(jnp.where(sel, iota, x.shape[0]), axis=0, keepdims=True)
    return m, sel


def _top16(arrays, break_ties):
    n, L = arrays[0].shape
    iota = lax.broadcasted_iota(jnp.int32, (n, L), 0)
    row16 = lax.broadcasted_iota(jnp.int32, (PEER_TOPK, L), 0)

    def one(j, s, rank, vals):
        m, sel = _pick_max(s, iota, break_ties)
        rank = jnp.where(sel, lax.convert_element_type(j, F32), rank)
        return jnp.where(sel, -jnp.inf, s), rank, jnp.where(row16 == j, m, vals)

    def body(j, carry):
        return tuple(one(j, *c) for c in carry)

    init = tuple((s, jnp.full((n, L), float(PEER_TOPK), F32), jnp.zeros((PEER_TOPK, L), F32)) for s in arrays)
    out = lax.fori_loop(0, PEER_TOPK, body, init)
    return [(rank, vals) for _, rank, vals in out]


_CAND_ROWS = (16,) + (8,) * 7


def _peer_route_kernel(h2t_ref, wq_ref, k1_ref, k2_ref, rank2_ref, e2_ref, cnt_ref, e1_ref, qt_scr):
    qt_scr[...] = _mm(wq_ref[...], h2t_ref[...])
    L = qt_scr.shape[1]
    dk = k1_ref.shape[2]
    ncand = sum(_CAND_ROWS) + 8
    iota_c = lax.broadcasted_iota(jnp.int32, (ncand, L), 0)
    row8 = lax.broadcasted_iota(jnp.int32, (8, L), 0)

    def head(h, _):
        base = pl.multiple_of(h * 2 * dk, 2 * dk)
        s1 = _mm(k1_ref[h], qt_scr[pl.ds(base, dk), :].astype(BF16))
        s2 = _mm(k2_ref[h], qt_scr[pl.ds(base + dk, dk), :].astype(BF16))
        def select(break_ties):
            (rank1, v1), (rank2, v2) = _top16((s1, s2), break_ties)
            cand = jnp.concatenate(
                [v1[r:r + 1] + v2[0:n] for r, n in enumerate(_CAND_ROWS)] + [v1[8:16] + v2[0:1]], axis=0)
            m0 = v1[0:1] + v2[0:1]

            def pick(j, carry):
                cand, chosen, z = carry
                m, sel = _pick_max(cand, iota_c, break_ties)
                return jnp.where(sel, -jnp.inf, cand), jnp.where(sel, 1.0, chosen), z + jnp.exp(m - m0)

            _, chosen, z = lax.fori_loop(0, PEER_TOPK, pick,
                                         (cand, jnp.zeros((ncand, L), F32), jnp.zeros((1, L), F32)))
            return rank1, rank2, v1[0:1], v2[0:1], chosen, z

        fast = select(False)
        picked = [jnp.sum(jnp.where(r < float(PEER_TOPK), 1.0, 0.0), axis=0, keepdims=True) for r in fast[:2]]
        picked.append(jnp.sum(fast[4], axis=0, keepdims=True))
        excess = functools.reduce(jnp.maximum, picked)
        rank1, rank2, top1, top2, chosen, z = lax.cond(
            jnp.max(excess) > float(PEER_TOPK), lambda: select(True), lambda: fast)
        low = jnp.zeros((8, L), F32)
        off = 0
        for r, n in enumerate(_CAND_ROWS):
            low = jnp.where(row8 == r, jnp.sum(chosen[off:off + n], axis=0, keepdims=True), low)
            off += n
        cnt16 = jnp.concatenate([low, chosen[off:off + 8]], axis=0)
        cnt = jnp.zeros_like(s1)
        for r in range(PEER_TOPK):
            cnt = jnp.where(rank1 == float(r), cnt16[r:r + 1], cnt)
        rank2_ref[h] = rank2
        e2_ref[h] = jnp.exp(s2 - top2)
        cnt_ref[h] = cnt
        e1_ref[h] = jnp.exp(s1 - top1) / z
        return 0

    lax.fori_loop(0, k1_ref.shape[0], head, 0)


def _peer_route_call(h2t, wq_t, k1, k2):
    D, N = h2t.shape
    heads, keys, dk = k1.shape
    tn = LANES
    tbl = pl.BlockSpec((heads, keys, tn), lambda i: (0, 0, i))
    shp = jax.ShapeDtypeStruct((heads, keys, N), F32)
    return pl.pallas_call(
        _peer_route_kernel,
        grid=(N // tn,),
        in_specs=[pl.BlockSpec((D, tn), lambda i: (0, i)),
                  pl.BlockSpec(wq_t.shape, lambda i: (0, 0)),
                  pl.BlockSpec(k1.shape, lambda i: (0, 0, 0)),
                  pl.BlockSpec(k2.shape, lambda i: (0, 0, 0))],
        out_specs=[tbl, tbl, tbl, tbl],
        out_shape=[shp, shp, shp, shp],
        scratch_shapes=[pltpu.VMEM((wq_t.shape[0], tn), F32)],
        compiler_params=_cparams(("parallel",)),
        name="peer_route",
    )(h2t, wq_t, k1, k2)


def _peer_dense_kernel(x_ref, g2_ref, h2t_ref, u_ref, vt_ref, rank2_ref, e2_ref, cnt_ref, e1_ref, o_ref,
                       acc_scr, rank2_scr, e2_scr, *, te):
    et = pl.program_id(2)
    heads, keys, _ = rank2_ref.shape

    @pl.when(et == 0)
    def _():
        acc_scr[...] = jnp.zeros_like(acc_scr)
        rank2_scr[...] = rank2_ref[...].astype(BF16)
        e2_scr[...] = e2_ref[...].astype(BF16)

    tn = h2t_ref.shape[1]
    n_i1 = te // keys
    zero = jnp.zeros((), BF16)
    sub = 16

    def pre_activation(rows, lanes):
        return _mm(u_ref[rows, :], h2t_ref[:, lanes])

    def routed(j0, n_j, lanes, a):
        act = (a * (1.0 + lax.erf(a * (2.0 ** -0.5)))).astype(BF16)
        width = act.shape[1]

        def row_bf16(ref, h, i1):
            return jnp.broadcast_to(ref[h, pl.ds(i1, 1), lanes], (sub, width)).astype(BF16)

        pieces = []
        for j in range(n_j):
            i1 = et * n_i1 + j0 + j
            cnt_rows = [row_bf16(cnt_ref, h, i1) for h in range(heads)]
            e1_rows = [row_bf16(e1_ref, h, i1) for h in range(heads)]
            for r in range(keys // sub):
                rs = slice(r * sub, (r + 1) * sub)
                g = jnp.zeros((sub, width), BF16)
                for h in range(heads):
                    g = g + jnp.where(rank2_scr[h, rs, lanes] < cnt_rows[h], e1_rows[h] * e2_scr[h, rs, lanes], zero)
                pieces.append(g * act[j * keys + r * sub:j * keys + (r + 1) * sub])
        return jnp.concatenate(pieces, axis=0)

    group = 2 * LANES if tn % (2 * LANES) == 0 else tn
    n_chunks = 4 if n_i1 % 4 == 0 else 1
    cj = n_i1 // n_chunks
    blocks = [(c, slice(t0, t0 + group)) for t0 in range(0, tn, group) for c in range(n_chunks)]
    pre = [pre_activation(slice(c * cj * keys, (c + 1) * cj * keys), lanes) for c, lanes in blocks]
    for (c, lanes), a in zip(blocks, pre):
        cols = slice(c * cj * keys, (c + 1) * cj * keys)
        acc_scr[:, lanes] += _mm(vt_ref[:, cols], routed(c * cj, cj, lanes, a))

    @pl.when(et == pl.num_programs(2) - 1)
    def _():
        o_ref[...] = x_ref[...] + g2_ref[...] * acc_scr[...].T


def _peer_dense_call(x, mod, h2t, u_b, vt_b, tables, tn, te):
    G, T, D = x.shape
    E = u_b.shape[0]
    R = mod.shape[1]
    rb = 1 if R == 1 else tn
    nt = T // tn
    heads, keys, _ = tables[0].shape
    tbl = pl.BlockSpec((heads, keys, tn), lambda b, i, e: (0, 0, b * nt + i))
    tok = pl.BlockSpec((None, tn, D), lambda b, i, e: (b, i, 0))
    return pl.pallas_call(
        functools.partial(_peer_dense_kernel, te=te),
        grid=(G, nt, E // te),
        in_specs=[tok, pl.BlockSpec((None, rb, D), lambda b, i, e: (b, 0 if R == 1 else i, 5)),
                  pl.BlockSpec((D, tn), lambda b, i, e: (0, b * nt + i)),
                  pl.BlockSpec((te, D), lambda b, i, e: (e, 0)),
                  pl.BlockSpec((D, te), lambda b, i, e: (0, e)),
                  tbl, tbl, tbl, tbl],
        out_specs=tok,
        out_shape=jax.ShapeDtypeStruct((G, T, D), F32),
        scratch_shapes=[pltpu.VMEM((D, tn), F32),
                        pltpu.VMEM((heads, keys, tn), BF16), pltpu.VMEM((heads, keys, tn), BF16)],
        compiler_params=_cparams(("parallel", "parallel", "arbitrary")),
        name="peer_dense",
    )(x, mod, h2t, u_b, vt_b, *tables)


def _layer(l, x, mod, page_table, caches, s0T, lb_row, lam_total, lam_init, lw, sizes):
    tm, tq_sb, tq_df, chunk, tn_peer, te_peer = sizes
    G, T, D = x.shape
    z = _inproj_call(x, lw["norm1_g"], mod, lw["w_in"], tm)
    if caches is None:
        o_sb = _sb_prompt_call(z["qsb"], z["ksb_b"], z["vsb_b"], tq_sb)
        o_df = _df_prompt_call(lam_total, z["qdf_t"], z["kdf_b"], z["vdf_t"], tq_df)
        tok = lambda a: a
    else:
        B = page_table.shape[0]
        per = lambda a: a.reshape(B, T // B, a.shape[-1])
        tok = lambda a: a.reshape(1, T, a.shape[-1])
        ck_sb, cv_sb, ck_df, cv_df = caches
        o_sb = tok(_sb_sample_call(page_table, per(z["qsb"]), per(z["ksb_b"]), per(z["vsb_b"]), ck_sb, cv_sb, l))
        o_df = tok(_df_sample_call(page_table, lam_total, per(z["qdf"]), per(z["kdf_b"]), per(z["vdf_b"]),
                                   ck_df, cv_df, l))
        z = dict(z, qhg=per(z["qhg"]), ahg=per(z["ahg"]), ihg=per(z["ihg"]))
    o_hg, sT = _hgrn_call(z["qhg"], z["ahg"], z["ihg"], lb_row, s0T, chunk)
    x, h2t = _merge_call(x, o_sb, tok(o_hg), z["ghg"], o_df, lw["gn_sb"], lw["gn_hg"], lw["gn_df"], lw["w_out"],
                         mod, lw["norm2_g"], 1.0 - lam_init, tm)
    tables = _peer_route_call(h2t, lw["w_pq_t"], lw["peer_k1"], lw["peer_k2"])
    x = _peer_dense_call(x, mod, h2t, lw["peer_u"], lw["peer_v_t"], tables, tn_peer, te_peer)
    return x, (z["ksb"], z["vsb"], z["kdf"], z["vdf"], sT)


def kernel(x_prompt, x_sample, cache_sb_k, cache_sb_v, cache_df_k, cache_df_v, state_hgrn, page_table,
           c_prompt, c_sample, w_ada, b_ada, norm1_g, norm2_g, w_in, lb_logits, lam_q1, lam_k1, lam_q2, lam_k2,
           gn_sb, gn_hg, gn_df, w_out, w_pq, peer_k1, peer_k2, peer_u, peer_v, final_g):
    depth = w_ada.shape[0]
    Bp, Tp, D = x_prompt.shape
    Bs, Ts, _ = x_sample.shape
    n_pool = cache_sb_k.shape[1]

    c_all = jnp.concatenate([c_prompt, c_sample], axis=0)
    pad = (-c_all.shape[0]) % 8
    mod_all = _mod_call(jnp.pad(c_all, ((0, pad), (0, 0))), w_ada, b_ada)
    lb_all, lam_diff = _prep_call(lb_logits, lam_q1, lam_k1, lam_q2, lam_k2)

    caches = (jnp.transpose(cache_sb_k, (0, 1, 3, 4, 2)), jnp.transpose(cache_sb_v, (0, 1, 3, 4, 2)),
              cache_df_k.reshape(depth, n_pool, PAGE * H_DF, 2 * DH_DF),
              cache_df_v.reshape(depth, n_pool, PAGE * H_DF, DV_DF))
    s0T_sample = jnp.swapaxes(state_hgrn, 3, 4)
    s0T_prompt = jnp.zeros((Bp, H_HG, DV_HG, DK_HG), F32)

    xp = x_prompt
    xs = x_sample.reshape(1, Bs * Ts, D)
    sizes_p = (512, 256, 256, 128, 512, 1024)
    sizes_s = (Bs * Ts, None, None, Ts, Bs * Ts, 1024)
    outs_p, outs_s = [], []
    for l in range(depth):
        lam_init = 0.8 - 0.6 * math.exp(-0.3 * l)
        lw = dict(norm1_g=norm1_g[l], norm2_g=norm2_g[l], w_in=w_in[l].astype(BF16), gn_sb=gn_sb[l], gn_hg=gn_hg[l],
                  gn_df=gn_df[l], w_out=w_out[l].astype(BF16), w_pq_t=w_pq[l].T.astype(BF16),
                  peer_k1=peer_k1[l].astype(BF16), peer_k2=peer_k2[l].astype(BF16),
                  peer_u=peer_u[l].astype(BF16), peer_v_t=(0.5 * peer_v[l]).T.astype(BF16))
        lam_total = (lam_diff[l, 0] + lam_init).reshape(1, 1)
        lb_row = lb_all[l].reshape(1, NK_HG)
        mod_p = mod_all[l, :Bp].reshape(Bp, 1, 6 * D)
        mod_s = jnp.repeat(mod_all[l, Bp:Bp + Bs], Ts, axis=0).reshape(1, Bs * Ts, 6 * D)
        xs, os_ = _layer(l, xs, mod_s, page_table, caches, s0T_sample[l], lb_row, lam_total, lam_init, lw, sizes_s)
        xp, op = _layer(l, xp, mod_p, None, None, s0T_prompt, lb_row, lam_total, lam_init, lw, sizes_p)
        outs_p.append(op)
        outs_s.append(os_)

    y_prompt = _final_norm_call(xp, final_g, 512)
    y_sample = _final_norm_call(xs, final_g, Bs * Ts).reshape(Bs, Ts, D)

    def stack(outs, B, T):
        ksb = jnp.stack([o[0] for o in outs]).reshape(depth, B, T, H_SB, DH_SB)
        vsb = jnp.stack([o[1] for o in outs]).reshape(depth, B, T, H_SB, DH_SB)
        kdf = jnp.stack([o[2] for o in outs]).reshape(depth, B, T, H_DF, 2 * DH_DF)
        vdf = jnp.stack([o[3] for o in outs]).reshape(depth, B, T, H_DF, DV_DF)
        st = jnp.swapaxes(jnp.stack([o[4] for o in outs]), 3, 4)
        return ksb, vsb, kdf, vdf, st

    return (y_prompt, y_sample) + stack(outs_p, Bp, Tp) + stack(outs_s, Bs, Ts)
```

```python
import functools
import math

import numpy as np
import jax
import jax.numpy as jnp
from jax import lax
from jax.experimental import pallas as pl
from jax.experimental.pallas import tpu as pltpu

F32 = jnp.float32
BF16 = jnp.bfloat16
EPS = 1e-6
LANES = 128
VMEM_LIMIT = 48 * 1024 * 1024
SB_DEAD = -120.0

H_SB, DH_SB = 4, 64
H_HG, DK_HG, DV_HG = 4, 128, 64
H_DF, DH_DF, DV_DF = 4, 64, 128
W_SB, W_HG, W_DF = H_SB * DH_SB, H_HG * DV_HG, H_DF * DV_DF
NK_HG = H_HG * DK_HG
NQK_DF = H_DF * 2 * DH_DF
PEER_HEADS, PEER_NKEYS, PEER_TOPK = 8, 128, 16
PAGE = 128


def _cparams(sem):
    return pltpu.CompilerParams(dimension_semantics=sem, vmem_limit_bytes=VMEM_LIMIT)


def _mm(a, b):
    return lax.dot_general(a, b, (((1,), (0,)), ((), ())), preferred_element_type=F32)


def _mm_nt(a, b):
    return lax.dot_general(a, b, (((1,), (1,)), ((), ())), preferred_element_type=F32)


def _mm_tn(a, b):
    return lax.dot_general(a, b, (((0,), (0,)), ((), ())), preferred_element_type=F32)


def _split(x):
    hi = x.astype(BF16)
    lo = (x - hi.astype(F32)).astype(BF16)
    return hi, lo


def _mm_f32_lhs(x, m_bf16):
    hi, lo = _split(x)
    return _mm(hi, m_bf16) + _mm(lo, m_bf16)


def _mm_f32_rhs(m_bf16, x):
    hi, lo = _split(x)
    return _mm(m_bf16, hi) + _mm(m_bf16, lo)


def _log_sigmoid_pair(z):
    l1p = jnp.log1p(jnp.exp(-jnp.abs(z)))
    return jnp.minimum(z, 0.0) - l1p, -(jnp.maximum(z, 0.0) + l1p)


def _mod_kernel(c_ref, w_ref, b_ref, o_ref):
    c = c_ref[...]
    a = c * (1.0 / (1.0 + jnp.exp(-c)))
    ahi, alo = _split(a)
    w = w_ref[...]
    whi, wlo = _split(w)
    o_ref[...] = _mm(ahi, whi) + _mm(ahi, wlo) + _mm(alo, whi) + b_ref[...]


def _mod_call(c_all, w_ada, b_ada):
    depth, d, d6 = w_ada.shape
    r = c_all.shape[0]
    tn = 1024
    return pl.pallas_call(
        _mod_kernel,
        grid=(depth, d6 // tn),
        in_specs=[pl.BlockSpec((r, d), lambda l, j: (0, 0)),
                  pl.BlockSpec((None, d, tn), lambda l, j: (l, 0, j)),
                  pl.BlockSpec((None, 1, tn), lambda l, j: (l, 0, j))],
        out_specs=pl.BlockSpec((None, r, tn), lambda l, j: (l, 0, j)),
        out_shape=jax.ShapeDtypeStruct((depth, r, d6), F32),
        compiler_params=_cparams(("parallel", "parallel")),
        name="adaln_mod",
    )(c_all, w_ada, b_ada.reshape(depth, 1, d6))


def _prep_kernel(lb_ref, q1_ref, k1_ref, q2_ref, k2_ref, lbo_ref, lam_ref):
    lg = lb_ref[...]
    depth = lg.shape[0]
    mx = jnp.max(lg, axis=0, keepdims=True)
    e = jnp.exp(lg - mx)
    p = e / jnp.sum(e, axis=0, keepdims=True)
    run = p[0:1]
    rows = [jnp.zeros_like(run)]
    for l in range(1, depth):
        run = run + p[l:l + 1]
        rows.append(run - p[0:1])
    lbo_ref[...] = jnp.concatenate(rows, axis=0)
    s1 = jnp.sum(q1_ref[...] * k1_ref[...], axis=1, keepdims=True)
    s2 = jnp.sum(q2_ref[...] * k2_ref[...], axis=1, keepdims=True)
    lam_ref[...] = jnp.broadcast_to(jnp.exp(s1) - jnp.exp(s2), lam_ref.shape)


def _prep_call(lb_logits, lq1, lk1, lq2, lk2):
    depth, n = lb_logits.shape
    return pl.pallas_call(
        _prep_kernel,
        out_shape=(jax.ShapeDtypeStruct((depth, n), F32),
                   jax.ShapeDtypeStruct((depth, LANES), F32)),
        name="layer_scalars",
    )(lb_logits, lq1, lk1, lq2, lk2)


_IN_SEGS = (
    ("qsb", W_SB, BF16, DH_SB ** -0.5), ("ksb", W_SB, F32, None), ("vsb", W_SB, F32, None),
    ("qhg", NK_HG, F32, None), ("ahg", NK_HG, F32, None), ("ihg", W_HG, BF16, None), ("ghg", W_HG, F32, None),
    ("qdf", NQK_DF, BF16, DH_DF ** -0.5 * math.log2(math.e)), ("kdf", NQK_DF, F32, None), ("vdf", W_DF, F32, None),
)
_BF16_COPIES = ("ksb", "vsb", "kdf", "vdf")
_T_COPIES = ("qdf", "vdf")
_IN_NAMES = [s[0] for s in _IN_SEGS] + [n + "_b" for n in _BF16_COPIES] + [n + "_t" for n in _T_COPIES]


def _inproj_kernel(x_ref, g_ref, sc_ref, sh_ref, w_ref, *out_refs):
    x = x_ref[...]
    y = x * lax.rsqrt(jnp.mean(x * x, axis=-1, keepdims=True) + EPS) * g_ref[...]
    h = (y * (1.0 + sc_ref[...]) + sh_ref[...]).astype(BF16)
    outs = dict(zip(_IN_NAMES, out_refs))
    off = 0
    for name, width, dt, scale in _IN_SEGS:
        z = _mm(h, w_ref[:, off:off + width])
        off += width
        if scale is not None:
            z = z * scale
        outs[name][...] = z.astype(dt)
        if name in _BF16_COPIES:
            outs[name + "_b"][...] = z.astype(BF16)
        if name in _T_COPIES:
            outs[name + "_t"][...] = z.T.astype(BF16)


def _inproj_call(x, g, mod, w_in_b, tm):
    G, T, D = x.shape
    R = mod.shape[1]
    rb = 1 if R == 1 else tm
    nt = T // tm
    width = w_in_b.shape[1]
    mrow = (lambda comp: (lambda b, i: (b, 0 if R == 1 else i, comp)))
    names = _IN_NAMES
    widths = {s[0]: s[1] for s in _IN_SEGS}
    dts = {s[0]: s[2] for s in _IN_SEGS}
    for n in _BF16_COPIES:
        widths[n + "_b"], dts[n + "_b"] = widths[n], BF16

    def out_spec(n):
        if n.endswith("_t"):
            return pl.BlockSpec((None, widths[n[:-2]], tm), lambda b, i: (b, 0, i))
        return pl.BlockSpec((None, tm, widths[n]), lambda b, i: (b, i, 0))

    def out_shape(n):
        if n.endswith("_t"):
            return jax.ShapeDtypeStruct((G, widths[n[:-2]], T), BF16)
        return jax.ShapeDtypeStruct((G, T, widths[n]), dts[n])

    outs = pl.pallas_call(
        _inproj_kernel,
        grid=(G, nt),
        in_specs=[pl.BlockSpec((None, tm, D), lambda b, i: (b, i, 0)),
                  pl.BlockSpec((1, D), lambda b, i: (0, 0)),
                  pl.BlockSpec((None, rb, D), mrow(1)),
                  pl.BlockSpec((None, rb, D), mrow(0)),
                  pl.BlockSpec((D, width), lambda b, i: (0, 0))],
        out_specs=[out_spec(n) for n in names],
        out_shape=[out_shape(n) for n in names],
        compiler_params=_cparams(("parallel", "parallel")),
        name="norm_inproj",
    )(x, g.reshape(1, D), mod, mod, w_in_b)
    return dict(zip(names, outs))


def _sb_block(q, k, v, c, acc, u, mask, transposed=False):
    z = _mm(q, k) if transposed else _mm_nt(q, k)
    ls_pos, ls_neg = _log_sigmoid_pair(z)
    if mask is not None:
        ls_neg = jnp.where(mask, ls_neg, 0.0)
    surv = _mm_f32_lhs(ls_neg, u) + c
    w = jnp.exp(ls_pos + surv)
    if mask is not None:
        w = jnp.where(mask, w, 0.0)
    w = w.astype(BF16)
    acc = acc + (_mm_nt(w, v) if transposed else _mm(w, v))
    c = c + jnp.sum(ls_neg, axis=1, keepdims=True)
    return c, acc


def _sb_prompt_kernel(q_ref, k_ref, v_ref, u_ref, o_ref, *, tq):
    qi = pl.program_id(1)
    u = u_ref[...]
    row = lax.broadcasted_iota(jnp.int32, (tq, tq), 0)
    col = lax.broadcasted_iota(jnp.int32, (tq, tq), 1)
    diag_mask = col < row
    heads = [slice(h * DH_SB, (h + 1) * DH_SB) for h in range(H_SB)]
    qs = [q_ref[:, hs] for hs in heads]

    def visit(kb, state, mask):
        st = pl.multiple_of(kb * tq, tq)
        return tuple(_sb_block(q, k_ref[pl.ds(st, tq), hs], v_ref[pl.ds(st, tq), hs], c, acc, u, mask)
                     for q, hs, (c, acc) in zip(qs, heads, state))

    init = tuple((jnp.zeros((tq, 1), F32), jnp.zeros((tq, DH_SB), F32)) for _ in heads)
    state = visit(qi, init, diag_mask)

    def cond(carry):
        kb, state = carry
        alive = functools.reduce(jnp.maximum, [c for c, _ in state])
        return jnp.logical_and(kb >= 0, jnp.max(alive) > SB_DEAD)

    def body(carry):
        kb, state = carry
        return kb - 1, visit(kb, state, None)

    _, state = lax.while_loop(cond, body, (qi - 1, state))
    for hs, (_, acc) in zip(heads, state):
        o_ref[:, hs] = acc


def _suffix_matrix(n):
    return jnp.asarray(np.tril(np.ones((n, n), np.float32), -1), BF16)


def _sb_prompt_call(q, k, v, tq):
    B, T, W = q.shape
    return pl.pallas_call(
        functools.partial(_sb_prompt_kernel, tq=tq),
        grid=(B, T // tq),
        in_specs=[pl.BlockSpec((None, tq, W), lambda b, i: (b, i, 0)),
                  pl.BlockSpec((None, T, W), lambda b, i: (b, 0, 0)),
                  pl.BlockSpec((None, T, W), lambda b, i: (b, 0, 0)),
                  pl.BlockSpec((tq, tq), lambda b, i: (0, 0))],
        out_specs=pl.BlockSpec((None, tq, W), lambda b, i: (b, i, 0)),
        out_shape=jax.ShapeDtypeStruct((B, T, W), F32),
        compiler_params=_cparams(("parallel", "parallel")),
        name="sb_prompt",
    )(q, k, v, _suffix_matrix(tq))


def _softmax_step(s, v, m, l, acc):
    m_new = jnp.maximum(m, jnp.max(s, axis=1, keepdims=True))
    alpha = jnp.exp2(m - m_new)
    p = jnp.exp2(s - m_new)
    l = alpha * l + jnp.sum(p, axis=1, keepdims=True)
    acc = alpha * acc + _mm(p.astype(BF16), v)
    return m_new, l, acc


_DF_GROUP = 4


def _df_prompt_kernel(lam_ref, qt_ref, k_ref, vt_ref, o_ref, *, tq):
    qi = pl.program_id(2)
    qt = qt_ref[...]
    first = lax.broadcasted_iota(jnp.int32, qt.shape, 0) < DH_DF
    zero = jnp.zeros((), BF16)
    qbd = jnp.concatenate([jnp.where(first, qt, zero), jnp.where(first, zero, qt)], axis=1)

    def scores(kb):
        return _mm(k_ref[pl.ds(pl.multiple_of(kb * tq, tq), tq), :], qbd)

    def update(kb, s, carry):
        m, l, acc = carry
        m_new = jnp.maximum(m, jnp.max(s, axis=0, keepdims=True))
        alpha = jnp.exp2(m - m_new)
        p = jnp.exp2(s - m_new)
        l = alpha * l + jnp.sum(p, axis=0, keepdims=True)
        acc = alpha * acc + _mm(vt_ref[:, pl.ds(pl.multiple_of(kb * tq, tq), tq)], p.astype(BF16))
        return m_new, l, acc

    key = lax.broadcasted_iota(jnp.int32, (tq, 2 * tq), 0)
    qry = lax.broadcasted_iota(jnp.int32, (tq, 2 * tq), 1) % tq
    init = (jnp.full((1, 2 * tq), -jnp.inf, F32), jnp.zeros((1, 2 * tq), F32), jnp.zeros((DV_DF, 2 * tq), F32))
    carry = update(qi, jnp.where(key <= qry, scores(qi), -jnp.inf), init)

    def group(j, carry):
        blocks = [_DF_GROUP * j + i for i in range(_DF_GROUP)]
        for kb, s in zip(blocks, [scores(kb) for kb in blocks]):
            carry = update(kb, s, carry)
        return carry

    n_groups = qi // _DF_GROUP
    carry = lax.fori_loop(0, n_groups, group, carry)
    _, l, acc = lax.fori_loop(n_groups * _DF_GROUP, qi, lambda kb, c: update(kb, scores(kb), c), carry)
    o = acc / l
    o_ref[...] = (o[:, :tq] - lam_ref[0, 0] * o[:, tq:]).T


def _df_prompt_call(lam, q_t, k, v_t, tq):
    B, T, _ = k.shape
    blk = 2 * DH_DF
    return pl.pallas_call(
        functools.partial(_df_prompt_kernel, tq=tq),
        grid=(B, H_DF, T // tq),
        in_specs=[pl.BlockSpec(memory_space=pltpu.SMEM),
                  pl.BlockSpec((None, blk, tq), lambda b, h, i: (b, h, i)),
                  pl.BlockSpec((None, T, blk), lambda b, h, i: (b, 0, h)),
                  pl.BlockSpec((None, DV_DF, T), lambda b, h, i: (b, h, 0))],
        out_specs=pl.BlockSpec((None, tq, DV_DF), lambda b, h, i: (b, i, h)),
        out_shape=jax.ShapeDtypeStruct((B, T, W_DF), F32),
        compiler_params=_cparams(("parallel", "parallel", "parallel")),
        name="df_prompt",
    )(lam, q_t, k, v_t)


def _hgrn_consts(C):
    levels = int(math.log2(C))
    t = np.arange(C)
    tril = (t[:, None] >= t[None, :]).astype(np.float32)
    lmat, rmat, masks = [], [], []
    for lv in range(levels):
        w = C >> (lv + 1)
        blk = t // (2 * w)
        second = (t % (2 * w)) >= w
        mid = blk * 2 * w + w - 1
        j = t[None, :]
        lmat.append(((j > mid[:, None]) & (j <= t[:, None]) & second[:, None]).astype(np.float32))
        rmat.append(((j > t[:, None]) & (j <= mid[:, None]) & (~second)[:, None]).astype(np.float32))
        masks.append(((blk[:, None] == blk[None, :]) & second[:, None] & (~second)[None, :]).astype(np.float32))
    masks.append(np.eye(C, dtype=np.float32))
    lr = np.concatenate([tril] + lmat + rmat, axis=0)
    return jnp.asarray(lr, BF16), jnp.asarray(np.stack(masks), F32), levels


def _hgrn_kernel(q_ref, a_ref, i_ref, lb_ref, s0_ref, lr_ref, mk_ref, o_ref, sT_ref, st_scr, *, C, levels):
    ci = pl.program_id(1)

    @pl.when(ci == 0)
    def _():
        st_scr[...] = s0_ref[...]

    a = a_ref[...]
    lb = lb_ref[...]
    lsa, _ = _log_sigmoid_pair(a)
    x0 = jnp.log(lb)
    y0 = jnp.log1p(-lb) + lsa
    log_f = jnp.maximum(x0, y0) + jnp.log1p(jnp.exp(-jnp.abs(x0 - y0)))
    kk = (1.0 - lb) / (1.0 + jnp.exp(a))
    sums = _mm_f32_rhs(lr_ref[...], log_f)
    b = sums[:C]
    q = q_ref[...]
    q_in = (q * jnp.exp(b)).astype(BF16)
    k_out = (kk * jnp.exp(b[C - 1:C] - b)).astype(BF16)
    qb = q.astype(BF16)
    kb = kk.astype(BF16)
    dec_l = jnp.exp(b[C - 1:C])
    for h in range(H_HG):
        ks = slice(h * DK_HG, (h + 1) * DK_HG)
        vs = slice(h * DV_HG, (h + 1) * DV_HG)
        att = mk_ref[levels] * _mm_nt(qb[:, ks], kb[:, ks])
        for lv in range(levels):
            dq = sums[(1 + lv) * C:(2 + lv) * C, ks]
            dk = sums[(1 + levels + lv) * C:(2 + levels + lv) * C, ks]
            ql = (q[:, ks] * jnp.exp(dq)).astype(BF16)
            kl = (kk[:, ks] * jnp.exp(dk)).astype(BF16)
            att = att + mk_ref[lv] * _mm_nt(ql, kl)
        iv = i_ref[:, vs]
        sT = st_scr[h]
        o_ref[:, vs] = _mm_nt(q_in[:, ks], sT.astype(BF16)) + _mm(att.astype(BF16), iv)
        st_scr[h] = sT * dec_l[:, ks] + _mm_tn(iv, k_out[:, ks])

    @pl.when(ci == pl.num_programs(1) - 1)
    def _():
        sT_ref[...] = st_scr[...]


def _hgrn_call(q, a, iv, lb_row, s0T, C):
    B, T, _ = q.shape
    lr, masks, levels = _hgrn_consts(C)
    nlr = lr.shape[0]
    return pl.pallas_call(
        functools.partial(_hgrn_kernel, C=C, levels=levels),
        grid=(B, T // C),
        in_specs=[pl.BlockSpec((None, C, NK_HG), lambda b, i: (b, i, 0)),
                  pl.BlockSpec((None, C, NK_HG), lambda b, i: (b, i, 0)),
                  pl.BlockSpec((None, C, W_HG), lambda b, i: (b, i, 0)),
                  pl.BlockSpec((1, NK_HG), lambda b, i: (0, 0)),
                  pl.BlockSpec((None, H_HG, DV_HG, DK_HG), lambda b, i: (b, 0, 0, 0)),
                  pl.BlockSpec((nlr, C), lambda b, i: (0, 0)),
                  pl.BlockSpec((levels + 1, C, C), lambda b, i: (0, 0, 0))],
        out_specs=[pl.BlockSpec((None, C, W_HG), lambda b, i: (b, i, 0)),
                   pl.BlockSpec((None, H_HG, DV_HG, DK_HG), lambda b, i: (b, 0, 0, 0))],
        out_shape=[jax.ShapeDtypeStruct((B, T, W_HG), F32),
                   jax.ShapeDtypeStruct((B, H_HG, DV_HG, DK_HG), F32)],
        scratch_shapes=[pltpu.VMEM((H_HG, DV_HG, DK_HG), F32)],
        compiler_params=_cparams(("parallel", "arbitrary")),
        name="hgrn2",
    )(q, a, iv, lb_row, s0T, lr, masks)


def _page_copies(cache_k, cache_v, kbuf, vbuf, semk, semv, layer, pages, slot):
    cps = []
    for j, page in enumerate(pages):
        cps.append(pltpu.make_async_copy(cache_k.at[layer, page], kbuf.at[slot, j], semk.at[slot]))
        cps.append(pltpu.make_async_copy(cache_v.at[layer, page], vbuf.at[slot, j], semv.at[slot]))
    return cps


def _sb_sample_kernel(pt_ref, q_ref, kn_ref, vn_ref, u_ref, un_ref, ck_ref, cv_ref, o_ref,
                      kbuf, vbuf, semk, semv, *, layer, P, n_tok):
    b = pl.program_id(0)
    nb = pl.num_programs(0)
    n_pages = pt_ref.shape[1]
    nch = n_pages // P
    slot = b % 2

    def copies(bb, ch, sl):
        pages = [pt_ref[bb, n_pages - 1 - ch * P - j] for j in range(P)]
        return _page_copies(ck_ref, cv_ref, kbuf, vbuf, semk, semv, layer, pages, sl)

    @pl.when(b == 0)
    def _():
        for cp in copies(0, 0, 0):
            cp.start()

    @pl.when(b + 1 < nb)
    def _():
        for cp in copies(b + 1, 0, 1 - slot):
            cp.start()

    q = q_ref[...]
    rows = q.shape[0]
    nn = kn_ref.shape[0]
    t_of_row = lax.broadcasted_iota(jnp.int32, (rows, nn), 0) % n_tok
    j_new = lax.broadcasted_iota(jnp.int32, (rows, nn), 1)
    mask_new = jnp.logical_and(j_new < t_of_row, j_new < n_tok)
    c, acc = _sb_block(q, kn_ref[...], vn_ref[...], jnp.zeros((rows, 1), F32),
                       jnp.zeros((rows, W_SB), F32), un_ref[...], mask_new)
    u = u_ref[...]

    def chunk(sl, c, acc):
        for j in range(P):
            kt = kbuf[sl, j].reshape(W_SB, PAGE).astype(BF16)
            vt = vbuf[sl, j].reshape(W_SB, PAGE).astype(BF16)
            c, acc = _sb_block(q, kt, vt, c, acc, u, None, transposed=True)
        return c, acc

    for cp in copies(b, 0, slot):
        cp.wait()
    c, acc = chunk(slot, c, acc)

    def cond(carry):
        ch, c, _ = carry
        return jnp.logical_and(ch < nch, jnp.max(c) > SB_DEAD)

    def body(carry):
        ch, c, acc = carry
        cps = copies(b, ch, 2)
        for cp in cps:
            cp.start()
        for cp in cps:
            cp.wait()
        c, acc = chunk(2, c, acc)
        return ch + 1, c, acc

    _, _, acc = lax.while_loop(cond, body, (jnp.int32(1), c, acc))
    for h in range(H_SB):
        hs = slice(h * DH_SB, (h + 1) * DH_SB)
        o_ref[:, hs] = acc[h * n_tok:(h + 1) * n_tok, hs]


def _block_diag_queries(q, groups):
    B, T, W = q.shape
    d = W // groups
    gid_col = (jnp.arange(W) // d)[None, None, None, :]
    gid_row = jnp.arange(groups)[None, :, None, None]
    out = jnp.where(gid_col == gid_row, q[:, None, :, :], jnp.zeros((), q.dtype))
    return out.reshape(B, groups * T, W)


def _pad_rows(x, n):
    return jnp.pad(x, ((0, 0), (0, n - x.shape[1]), (0, 0)))


_NEW_PAD = 16


def _sb_sample_call(page_table, q, k_new, v_new, cache_k, cache_v, layer, P=4):
    B, n_tok, W = q.shape
    page_shape = cache_k.shape[2:]
    qbd = _block_diag_queries(q, H_SB)
    rows = qbd.shape[1]
    kn, vn = _pad_rows(k_new, _NEW_PAD), _pad_rows(v_new, _NEW_PAD)
    grid_spec = pltpu.PrefetchScalarGridSpec(
        num_scalar_prefetch=1,
        grid=(B,),
        in_specs=[pl.BlockSpec((None, rows, W), lambda b, pt: (b, 0, 0)),
                  pl.BlockSpec((None, _NEW_PAD, W), lambda b, pt: (b, 0, 0)),
                  pl.BlockSpec((None, _NEW_PAD, W), lambda b, pt: (b, 0, 0)),
                  pl.BlockSpec((PAGE, PAGE), lambda b, pt: (0, 0)),
                  pl.BlockSpec((_NEW_PAD, _NEW_PAD), lambda b, pt: (0, 0)),
                  pl.BlockSpec(memory_space=pl.ANY),
                  pl.BlockSpec(memory_space=pl.ANY)],
        out_specs=pl.BlockSpec((None, n_tok, W), lambda b, pt: (b, 0, 0)),
        scratch_shapes=[pltpu.VMEM((3, P) + page_shape, F32), pltpu.VMEM((3, P) + page_shape, F32),
                        pltpu.SemaphoreType.DMA((3,)), pltpu.SemaphoreType.DMA((3,))])
    return pl.pallas_call(
        functools.partial(_sb_sample_kernel, layer=layer, P=P, n_tok=n_tok),
        grid_spec=grid_spec,
        out_shape=jax.ShapeDtypeStruct((B, n_tok, W), F32),
        compiler_params=_cparams(("arbitrary",)),
        name="sb_sample",
    )(page_table, qbd, kn, vn, _suffix_matrix(PAGE), _suffix_matrix(_NEW_PAD), cache_k, cache_v)


def _df_sample_kernel(pt_ref, lam_ref, q_ref, kn_ref, vn_ref, ck_ref, cv_ref, o_ref,
                      kbuf, vbuf, semk, semv, m_scr, l_scr, acc_scr, *, layer, P, n_tok):
    g = pl.program_id(0)
    total = pl.num_programs(0)
    n_pages = pt_ref.shape[1]
    nch = n_pages // P
    c = g % nch
    slot = g % 2

    page_rows = PAGE * H_DF

    def copies(gg, sl):
        bb = gg // nch
        cc = gg % nch
        cps = []
        for j in range(P):
            page = pt_ref[bb, cc * P + j]
            dst = pl.ds(j * page_rows, page_rows)
            cps.append(pltpu.make_async_copy(ck_ref.at[layer, page], kbuf.at[sl, dst], semk.at[sl]))
            cps.append(pltpu.make_async_copy(cv_ref.at[layer, page], vbuf.at[sl, dst], semv.at[sl]))
        return cps

    @pl.when(g == 0)
    def _():
        for cp in copies(0, 0):
            cp.start()

    @pl.when(g + 1 < total)
    def _():
        for cp in copies(g + 1, 1 - slot):
            cp.start()

    rows = q_ref.shape[1]

    @pl.when(c == 0)
    def _():
        nn = kn_ref.shape[0]
        t_of_row = lax.broadcasted_iota(jnp.int32, (rows, nn), 0) % n_tok
        j_new = lax.broadcasted_iota(jnp.int32, (rows, nn), 1)
        for h in range(H_DF):
            hs = slice(h * DV_DF, (h + 1) * DV_DF)
            s = jnp.where(j_new <= t_of_row, _mm_nt(q_ref[h], kn_ref[:, hs]), -jnp.inf)
            m = jnp.max(s, axis=1, keepdims=True)
            p = jnp.exp2(s - m)
            m_scr[h] = m
            l_scr[h] = jnp.sum(p, axis=1, keepdims=True)
            acc_scr[h] = _mm(p.astype(BF16), vn_ref[:, hs])

    for cp in copies(g, slot):
        cp.wait()
    for h in range(H_DF):
        head_rows = pl.ds(h, P * PAGE, stride=H_DF)
        k = kbuf[slot, head_rows, :].astype(BF16)
        v = vbuf[slot, head_rows, :].astype(BF16)
        m, l, acc = _softmax_step(_mm_nt(q_ref[h], k), v, m_scr[h], l_scr[h], acc_scr[h])
        m_scr[h] = m
        l_scr[h] = l
        acc_scr[h] = acc

    @pl.when(c == nch - 1)
    def _():
        for h in range(H_DF):
            o = acc_scr[h] / l_scr[h]
            o_ref[:, h * DV_DF:(h + 1) * DV_DF] = o[:n_tok] - lam_ref[0, 0] * o[n_tok:]


def _df_sample_call(page_table, lam, q, k_new, v_new, cache_k, cache_v, layer, P=16):
    B, n_tok, W = q.shape
    qh = q.reshape(B, n_tok, H_DF, 2 * DH_DF).transpose(0, 2, 1, 3)
    map_of_col = (jnp.arange(2 * DH_DF) // DH_DF)[None, None, None, None, :]
    map_of_row = jnp.arange(2)[None, None, :, None, None]
    qbd = jnp.where(map_of_col == map_of_row, qh[:, :, None], jnp.zeros((), q.dtype))
    rows = 2 * n_tok
    qbd = qbd.reshape(B, H_DF, rows, 2 * DH_DF)
    kn, vn = _pad_rows(k_new, _NEW_PAD), _pad_rows(v_new, _NEW_PAD)
    n_pages = page_table.shape[1]
    nch = n_pages // P
    page_rows, lanes = cache_k.shape[2:]
    grid_spec = pltpu.PrefetchScalarGridSpec(
        num_scalar_prefetch=1,
        grid=(B * nch,),
        in_specs=[pl.BlockSpec(memory_space=pltpu.SMEM),
                  pl.BlockSpec((None, H_DF, rows, 2 * DH_DF), lambda g, pt: (g // nch, 0, 0, 0)),
                  pl.BlockSpec((None, _NEW_PAD, W), lambda g, pt: (g // nch, 0, 0)),
                  pl.BlockSpec((None, _NEW_PAD, W_DF), lambda g, pt: (g // nch, 0, 0)),
                  pl.BlockSpec(memory_space=pl.ANY),
                  pl.BlockSpec(memory_space=pl.ANY)],
        out_specs=pl.BlockSpec((None, n_tok, W_DF), lambda g, pt: (g // nch, 0, 0)),
        scratch_shapes=[pltpu.VMEM((2, P * page_rows, lanes), F32), pltpu.VMEM((2, P * page_rows, lanes), F32),
                        pltpu.SemaphoreType.DMA((2,)), pltpu.SemaphoreType.DMA((2,)),
                        pltpu.VMEM((H_DF, rows, 1), F32), pltpu.VMEM((H_DF, rows, 1), F32),
                        pltpu.VMEM((H_DF, rows, DV_DF), F32)])
    return pl.pallas_call(
        functools.partial(_df_sample_kernel, layer=layer, P=P, n_tok=n_tok),
        grid_spec=grid_spec,
        out_shape=jax.ShapeDtypeStruct((B, n_tok, W_DF), F32),
        compiler_params=_cparams(("arbitrary",)),
        name="df_sample",
    )(page_table, lam, qbd, kn, vn, cache_k, cache_v)


def _group_mean_matrix(width, d):
    g = np.arange(width) // d
    return jnp.asarray((g[:, None] == g[None, :]).astype(np.float32) / d, BF16)


def _head_norm(o, group_mean, gain):
    return o * lax.rsqrt(_mm_f32_lhs(o * o, group_mean) + EPS) * gain


def _rms_mod(x, g, sc, sh):
    y = x * lax.rsqrt(jnp.mean(x * x, axis=-1, keepdims=True) + EPS) * g
    return y * (1.0 + sc) + sh


def _merge_kernel(x_ref, osb_ref, ohg_ref, ghg_ref, odf_ref, gsb_ref, gnh_ref, gdf_ref, m64_ref, m128_ref,
                  w_ref, g1_ref, n2_ref, sc2_ref, sh2_ref, x_out, h2_out, *, df_scale):
    m64 = m64_ref[...]
    gate = ghg_ref[...]
    o_sb = _head_norm(osb_ref[...], m64, gsb_ref[...]).astype(BF16)
    o_hg = (_head_norm(ohg_ref[...], m64, gnh_ref[...]) * (gate / (1.0 + jnp.exp(-gate)))).astype(BF16)
    o_df = (_head_norm(odf_ref[...], m128_ref[...], gdf_ref[...]) * df_scale).astype(BF16)
    proj = (_mm(o_sb, w_ref[0:W_SB, :]) + _mm(o_hg, w_ref[W_SB:W_SB + W_HG, :])
            + _mm(o_df, w_ref[W_SB + W_HG:, :]))
    x = x_ref[...] + g1_ref[...] * proj
    x_out[...] = x
    h2_out[...] = _rms_mod(x, n2_ref[...], sc2_ref[...], sh2_ref[...]).T.astype(BF16)


def _merge_call(x, o_sb, o_hg, gate, o_df, gn_sb, gn_hg, gn_df, w_out_b, mod, norm2_g, df_scale, tm):
    G, T, D = x.shape
    R = mod.shape[1]
    rb = 1 if R == 1 else tm
    mrow = (lambda comp: (lambda b, i: (b, 0 if R == 1 else i, comp)))
    tok = lambda w: pl.BlockSpec((None, tm, w), lambda b, i: (b, i, 0))
    full = lambda a: pl.BlockSpec(a.shape, lambda b, i: (0,) * a.ndim)
    m64, m128 = _group_mean_matrix(W_SB, DH_SB), _group_mean_matrix(W_DF, DV_DF)
    gsb, ghg, gdf, n2 = gn_sb.reshape(1, -1), gn_hg.reshape(1, -1), gn_df.reshape(1, -1), norm2_g.reshape(1, D)
    return pl.pallas_call(
        functools.partial(_merge_kernel, df_scale=df_scale),
        grid=(G, T // tm),
        in_specs=[tok(D), tok(W_SB), tok(W_HG), tok(W_HG), tok(W_DF), full(gsb), full(ghg), full(gdf),
                  full(m64), full(m128), full(w_out_b),
                  pl.BlockSpec((None, rb, D), mrow(2)), full(n2),
                  pl.BlockSpec((None, rb, D), mrow(4)), pl.BlockSpec((None, rb, D), mrow(3))],
        out_specs=[tok(D), pl.BlockSpec((D, tm), lambda b, i: (0, b * (T // tm) + i))],
        out_shape=[jax.ShapeDtypeStruct((G, T, D), F32), jax.ShapeDtypeStruct((D, G * T), BF16)],
        compiler_params=_cparams(("parallel", "parallel")),
        name="merge_outproj",
    )(x, o_sb, o_hg, gate, o_df, gsb, ghg, gdf, m64, m128, w_out_b, mod, n2, mod, mod)


def _final_norm_kernel(x_ref, g_ref, o_ref):
    x = x_ref[...]
    o_ref[...] = x * lax.rsqrt(jnp.mean(x * x, axis=-1, keepdims=True) + EPS) * g_ref[...]


def _final_norm_call(x, g, tm):
    G, T, D = x.shape
    return pl.pallas_call(
        _final_norm_kernel,
        grid=(G, T // tm),
        in_specs=[pl.BlockSpec((None, tm, D), lambda b, i: (b, i, 0)), pl.BlockSpec((1, D), lambda b, i: (0, 0))],
        out_specs=pl.BlockSpec((None, tm, D), lambda b, i: (b, i, 0)),
        out_shape=jax.ShapeDtypeStruct((G, T, D), F32),
        compiler_params=_cparams(("parallel", "parallel")),
        name="final_norm",
    )(x, g.reshape(1, D))


def _pick_max(x, iota, break_ties):
    m = jnp.max(x, axis=0, keepdims=True)
    sel = x == m
    if break_ties:
        sel = iota == jnp.min(jnp.where(sel, iota, x.shape[0]), axis=0, keepdims=True)
    return m, sel


def _top16(arrays, break_ties):
    n, L = arrays[0].shape
    iota = lax.broadcasted_iota(jnp.int32, (n, L), 0)
    row16 = lax.broadcasted_iota(jnp.int32, (PEER_TOPK, L), 0)

    def one(j, s, rank, vals):
        m, sel = _pick_max(s, iota, break_ties)
        rank = jnp.where(sel, lax.convert_element_type(j, F32), rank)
        return jnp.where(sel, -jnp.inf, s), rank, jnp.where(row16 == j, m, vals)

    def body(j, carry):
        return tuple(one(j, *c) for c in carry)

    init = tuple((s, jnp.full((n, L), float(PEER_TOPK), F32), jnp.zeros((PEER_TOPK, L), F32)) for s in arrays)
    out = lax.fori_loop(0, PEER_TOPK, body, init)
    return [(rank, vals) for _, rank, vals in out]


_CAND_ROWS = (16,) + (8,) * 7


def _peer_route_kernel(h2t_ref, wq_ref, k1_ref, k2_ref, rank2_ref, e2_ref, cnt_ref, e1_ref, qt_scr):
    qt_scr[...] = _mm(wq_ref[...], h2t_ref[...])
    L = qt_scr.shape[1]
    dk = k1_ref.shape[2]
    ncand = sum(_CAND_ROWS) + 8
    iota_c = lax.broadcasted_iota(jnp.int32, (ncand, L), 0)
    row8 = lax.broadcasted_iota(jnp.int32, (8, L), 0)

    def head(h, _):
        base = pl.multiple_of(h * 2 * dk, 2 * dk)
        s1 = _mm(k1_ref[h], qt_scr[pl.ds(base, dk), :].astype(BF16))
        s2 = _mm(k2_ref[h], qt_scr[pl.ds(base + dk, dk), :].astype(BF16))
        def select(break_ties):
            (rank1, v1), (rank2, v2) = _top16((s1, s2), break_ties)
            cand = jnp.concatenate(
                [v1[r:r + 1] + v2[0:n] for r, n in enumerate(_CAND_ROWS)] + [v1[8:16] + v2[0:1]], axis=0)
            m0 = v1[0:1] + v2[0:1]

            def pick(j, carry):
                cand, chosen, z = carry
                m, sel = _pick_max(cand, iota_c, break_ties)
                return jnp.where(sel, -jnp.inf, cand), jnp.where(sel, 1.0, chosen), z + jnp.exp(m - m0)

            _, chosen, z = lax.fori_loop(0, PEER_TOPK, pick,
                                         (cand, jnp.zeros((ncand, L), F32), jnp.zeros((1, L), F32)))
            return rank1, rank2, v1[0:1], v2[0:1], chosen, z

        fast = select(False)
        picked = [jnp.sum(jnp.where(r < float(PEER_TOPK), 1.0, 0.0), axis=0, keepdims=True) for r in fast[:2]]
        picked.append(jnp.sum(fast[4], axis=0, keepdims=True))
        excess = functools.reduce(jnp.maximum, picked)
        rank1, rank2, top1, top2, chosen, z = lax.cond(
            jnp.max(excess) > float(PEER_TOPK), lambda: select(True), lambda: fast)
        low = jnp.zeros((8, L), F32)
        off = 0
        for r, n in enumerate(_CAND_ROWS):
            low = jnp.where(row8 == r, jnp.sum(chosen[off:off + n], axis=0, keepdims=True), low)
            off += n
        cnt16 = jnp.concatenate([low, chosen[off:off + 8]], axis=0)
        cnt = jnp.zeros_like(s1)
        for r in range(PEER_TOPK):
            cnt = jnp.where(rank1 == float(r), cnt16[r:r + 1], cnt)
        rank2_ref[h] = rank2
        e2_ref[h] = jnp.exp(s2 - top2)
        cnt_ref[h] = cnt
        e1_ref[h] = jnp.exp(s1 - top1) / z
        return 0

    lax.fori_loop(0, k1_ref.shape[0], head, 0)


def _peer_route_call(h2t, wq_t, k1, k2):
    D, N = h2t.shape
    heads, keys, dk = k1.shape
    tn = LANES
    tbl = pl.BlockSpec((heads, keys, tn), lambda i: (0, 0, i))
    shp = jax.ShapeDtypeStruct((heads, keys, N), F32)
    return pl.pallas_call(
        _peer_route_kernel,
        grid=(N // tn,),
        in_specs=[pl.BlockSpec((D, tn), lambda i: (0, i)),
                  pl.BlockSpec(wq_t.shape, lambda i: (0, 0)),
                  pl.BlockSpec(k1.shape, lambda i: (0, 0, 0)),
                  pl.BlockSpec(k2.shape, lambda i: (0, 0, 0))],
        out_specs=[tbl, tbl, tbl, tbl],
        out_shape=[shp, shp, shp, shp],
        scratch_shapes=[pltpu.VMEM((wq_t.shape[0], tn), F32)],
        compiler_params=_cparams(("parallel",)),
        name="peer_route",
    )(h2t, wq_t, k1, k2)


def _peer_dense_kernel(x_ref, g2_ref, h2t_ref, u_ref, vt_ref, rank2_ref, e2_ref, cnt_ref, e1_ref, o_ref,
                       acc_scr, rank2_scr, e2_scr, *, te):
    et = pl.program_id(2)
    heads, keys, _ = rank2_ref.shape

    @pl.when(et == 0)
    def _():
        acc_scr[...] = jnp.zeros_like(acc_scr)
        rank2_scr[...] = rank2_ref[...].astype(BF16)
        e2_scr[...] = e2_ref[...].astype(BF16)

    tn = h2t_ref.shape[1]
    n_i1 = te // keys
    zero = jnp.zeros((), BF16)
    sub = 16

    def pre_activation(rows, lanes):
        return _mm(u_ref[rows, :], h2t_ref[:, lanes])

    def routed(j0, n_j, lanes, a):
        act = (a * (1.0 + lax.erf(a * (2.0 ** -0.5)))).astype(BF16)
        width = act.shape[1]

        def row_bf16(ref, h, i1):
            return jnp.broadcast_to(ref[h, pl.ds(i1, 1), lanes], (sub, width)).astype(BF16)

        pieces = []
        for j in range(n_j):
            i1 = et * n_i1 + j0 + j
            cnt_rows = [row_bf16(cnt_ref, h, i1) for h in range(heads)]
            e1_rows = [row_bf16(e1_ref, h, i1) for h in range(heads)]
            for r in range(keys // sub):
                rs = slice(r * sub, (r + 1) * sub)
                g = jnp.zeros((sub, width), BF16)
                for h in range(heads):
                    g = g + jnp.where(rank2_scr[h, rs, lanes] < cnt_rows[h], e1_rows[h] * e2_scr[h, rs, lanes], zero)
                pieces.append(g * act[j * keys + r * sub:j * keys + (r + 1) * sub])
        return jnp.concatenate(pieces, axis=0)

    group = 2 * LANES if tn % (2 * LANES) == 0 else tn
    n_chunks = 4 if n_i1 % 4 == 0 else 1
    cj = n_i1 // n_chunks
    blocks = [(c, slice(t0, t0 + group)) for t0 in range(0, tn, group) for c in range(n_chunks)]
    pre = [pre_activation(slice(c * cj * keys, (c + 1) * cj * keys), lanes) for c, lanes in blocks]
    for (c, lanes), a in zip(blocks, pre):
        cols = slice(c * cj * keys, (c + 1) * cj * keys)
        acc_scr[:, lanes] += _mm(vt_ref[:, cols], routed(c * cj, cj, lanes, a))

    @pl.when(et == pl.num_programs(2) - 1)
    def _():
        o_ref[...] = x_ref[...] + g2_ref[...] * acc_scr[...].T


def _peer_dense_call(x, mod, h2t, u_b, vt_b, tables, tn, te):
    G, T, D = x.shape
    E = u_b.shape[0]
    R = mod.shape[1]
    rb = 1 if R == 1 else tn
    nt = T // tn
    heads, keys, _ = tables[0].shape
    tbl = pl.BlockSpec((heads, keys, tn), lambda b, i, e: (0, 0, b * nt + i))
    tok = pl.BlockSpec((None, tn, D), lambda b, i, e: (b, i, 0))
    return pl.pallas_call(
        functools.partial(_peer_dense_kernel, te=te),
        grid=(G, nt, E // te),
        in_specs=[tok, pl.BlockSpec((None, rb, D), lambda b, i, e: (b, 0 if R == 1 else i, 5)),
                  pl.BlockSpec((D, tn), lambda b, i, e: (0, b * nt + i)),
                  pl.BlockSpec((te, D), lambda b, i, e: (e, 0)),
                  pl.BlockSpec((D, te), lambda b, i, e: (0, e)),
                  tbl, tbl, tbl, tbl],
        out_specs=tok,
        out_shape=jax.ShapeDtypeStruct((G, T, D), F32),
        scratch_shapes=[pltpu.VMEM((D, tn), F32),
                        pltpu.VMEM((heads, keys, tn), BF16), pltpu.VMEM((heads, keys, tn), BF16)],
        compiler_params=_cparams(("parallel", "parallel", "arbitrary")),
        name="peer_dense",
    )(x, mod, h2t, u_b, vt_b, *tables)


def _layer(l, x, mod, page_table, caches, s0T, lb_row, lam_total, lam_init, lw, sizes):
    tm, tq_sb, tq_df, chunk, tn_peer, te_peer = sizes
    G, T, D = x.shape
    z = _inproj_call(x, lw["norm1_g"], mod, lw["w_in"], tm)
    if caches is None:
        o_sb = _sb_prompt_call(z["qsb"], z["ksb_b"], z["vsb_b"], tq_sb)
        o_df = _df_prompt_call(lam_total, z["qdf_t"], z["kdf_b"], z["vdf_t"], tq_df)
        tok = lambda a: a
    else:
        B = page_table.shape[0]
        per = lambda a: a.reshape(B, T // B, a.shape[-1])
        tok = lambda a: a.reshape(1, T, a.shape[-1])
        ck_sb, cv_sb, ck_df, cv_df = caches
        o_sb = tok(_sb_sample_call(page_table, per(z["qsb"]), per(z["ksb_b"]), per(z["vsb_b"]), ck_sb, cv_sb, l))
        o_df = tok(_df_sample_call(page_table, lam_total, per(z["qdf"]), per(z["kdf_b"]), per(z["vdf_b"]),
                                   ck_df, cv_df, l))
        z = dict(z, qhg=per(z["qhg"]), ahg=per(z["ahg"]), ihg=per(z["ihg"]))
    o_hg, sT = _hgrn_call(z["qhg"], z["ahg"], z["ihg"], lb_row, s0T, chunk)
    x, h2t = _merge_call(x, o_sb, tok(o_hg), z["ghg"], o_df, lw["gn_sb"], lw["gn_hg"], lw["gn_df"], lw["w_out"],
                         mod, lw["norm2_g"], 1.0 - lam_init, tm)
    tables = _peer_route_call(h2t, lw["w_pq_t"], lw["peer_k1"], lw["peer_k2"])
    x = _peer_dense_call(x, mod, h2t, lw["peer_u"], lw["peer_v_t"], tables, tn_peer, te_peer)
    return x, (z["ksb"], z["vsb"], z["kdf"], z["vdf"], sT)


def kernel(x_prompt, x_sample, cache_sb_k, cache_sb_v, cache_df_k, cache_df_v, state_hgrn, page_table,
           c_prompt, c_sample, w_ada, b_ada, norm1_g, norm2_g, w_in, lb_logits, lam_q1, lam_k1, lam_q2, lam_k2,
           gn_sb, gn_hg, gn_df, w_out, w_pq, peer_k1, peer_k2, peer_u, peer_v, final_g):
    depth = w_ada.shape[0]
    Bp, Tp, D = x_prompt.shape
    Bs, Ts, _ = x_sample.shape
    n_pool = cache_sb_k.shape[1]

    c_all = jnp.concatenate([c_prompt, c_sample], axis=0)
    pad = (-c_all.shape[0]) % 8
    mod_all = _mod_call(jnp.pad(c_all, ((0, pad), (0, 0))), w_ada, b_ada)
    lb_all, lam_diff = _prep_call(lb_logits, lam_q1, lam_k1, lam_q2, lam_k2)

    caches = (jnp.transpose(cache_sb_k, (0, 1, 3, 4, 2)), jnp.transpose(cache_sb_v, (0, 1, 3, 4, 2)),
              cache_df_k.reshape(depth, n_pool, PAGE * H_DF, 2 * DH_DF),
              cache_df_v.reshape(depth, n_pool, PAGE * H_DF, DV_DF))
    s0T_sample = jnp.swapaxes(state_hgrn, 3, 4)
    s0T_prompt = jnp.zeros((Bp, H_HG, DV_HG, DK_HG), F32)

    xp = x_prompt
    xs = x_sample.reshape(1, Bs * Ts, D)
    sizes_p = (512, 256, 512, 128, 512, 1024)
    sizes_s = (Bs * Ts, None, None, Ts, Bs * Ts, 1024)
    outs_p, outs_s = [], []
    for l in range(depth):
        lam_init = 0.8 - 0.6 * math.exp(-0.3 * l)
        lw = dict(norm1_g=norm1_g[l], norm2_g=norm2_g[l], w_in=w_in[l].astype(BF16), gn_sb=gn_sb[l], gn_hg=gn_hg[l],
                  gn_df=gn_df[l], w_out=w_out[l].astype(BF16), w_pq_t=w_pq[l].T.astype(BF16),
                  peer_k1=peer_k1[l].astype(BF16), peer_k2=peer_k2[l].astype(BF16),
                  peer_u=peer_u[l].astype(BF16), peer_v_t=(0.5 * peer_v[l]).T.astype(BF16))
        lam_total = (lam_diff[l, 0] + lam_init).reshape(1, 1)
        lb_row = lb_all[l].reshape(1, NK_HG)
        mod_p = mod_all[l, :Bp].reshape(Bp, 1, 6 * D)
        mod_s = jnp.repeat(mod_all[l, Bp:Bp + Bs], Ts, axis=0).reshape(1, Bs * Ts, 6 * D)
        xs, os_ = _layer(l, xs, mod_s, page_table, caches, s0T_sample[l], lb_row, lam_total, lam_init, lw, sizes_s)
        xp, op = _layer(l, xp, mod_p, None, None, s0T_prompt, lb_row, lam_total, lam_init, lw, sizes_p)
        outs_p.append(op)
        outs_s.append(os_)

    y_prompt = _final_norm_call(xp, final_g, 512)
    y_sample = _final_norm_call(xs, final_g, Bs * Ts).reshape(Bs, Ts, D)

    def stack(outs, B, T):
        ksb = jnp.stack([o[0] for o in outs]).reshape(depth, B, T, H_SB, DH_SB)
        vsb = jnp.stack([o[1] for o in outs]).reshape(depth, B, T, H_SB, DH_SB)
        kdf = jnp.stack([o[2] for o in outs]).reshape(depth, B, T, H_DF, 2 * DH_DF)
        vdf = jnp.stack([o[3] for o in outs]).reshape(depth, B, T, H_DF, DV_DF)
        st = jnp.swapaxes(jnp.stack([o[4] for o in outs]), 3, 4)
        return ksb, vsb, kdf, vdf, st

    return (y_prompt, y_sample) + stack(outs_p, Bp, Tp) + stack(outs_s, Bs, Ts)
```

```python
import functools
import math

import numpy as np
import jax
import jax.numpy as jnp
from jax import lax
from jax.experimental import pallas as pl
from jax.experimental.pallas import tpu as pltpu

F32 = jnp.float32
BF16 = jnp.bfloat16
EPS = 1e-6
LANES = 128
VMEM_LIMIT = 48 * 1024 * 1024
SB_DEAD = -120.0

H_SB, DH_SB = 4, 64
H_HG, DK_HG, DV_HG = 4, 128, 64
H_DF, DH_DF, DV_DF = 4, 64, 128
W_SB, W_HG, W_DF = H_SB * DH_SB, H_HG * DV_HG, H_DF * DV_DF
NK_HG = H_HG * DK_HG
NQK_DF = H_DF * 2 * DH_DF
PEER_HEADS, PEER_NKEYS, PEER_TOPK = 8, 128, 16
PAGE = 128


def _cparams(sem):
    return pltpu.CompilerParams(dimension_semantics=sem, vmem_limit_bytes=VMEM_LIMIT)


def _mm(a, b):
    return lax.dot_general(a, b, (((1,), (0,)), ((), ())), preferred_element_type=F32)


def _mm_nt(a, b):
    return lax.dot_general(a, b, (((1,), (1,)), ((), ())), preferred_element_type=F32)


def _mm_tn(a, b):
    return lax.dot_general(a, b, (((0,), (0,)), ((), ())), preferred_element_type=F32)


def _split(x):
    hi = x.astype(BF16)
    lo = (x - hi.astype(F32)).astype(BF16)
    return hi, lo


def _mm_f32_lhs(x, m_bf16):
    hi, lo = _split(x)
    return _mm(hi, m_bf16) + _mm(lo, m_bf16)


def _mm_f32_rhs(m_bf16, x):
    hi, lo = _split(x)
    return _mm(m_bf16, hi) + _mm(m_bf16, lo)


def _log_sigmoid_pair(z):
    l1p = jnp.log1p(jnp.exp(-jnp.abs(z)))
    return jnp.minimum(z, 0.0) - l1p, -(jnp.maximum(z, 0.0) + l1p)


def _mod_kernel(c_ref, w_ref, b_ref, o_ref):
    c = c_ref[...]
    a = c * (1.0 / (1.0 + jnp.exp(-c)))
    ahi, alo = _split(a)
    w = w_ref[...]
    whi, wlo = _split(w)
    o_ref[...] = _mm(ahi, whi) + _mm(ahi, wlo) + _mm(alo, whi) + b_ref[...]


def _mod_call(c_all, w_ada, b_ada):
    depth, d, d6 = w_ada.shape
    r = c_all.shape[0]
    tn = 1024
    return pl.pallas_call(
        _mod_kernel,
        grid=(depth, d6 // tn),
        in_specs=[pl.BlockSpec((r, d), lambda l, j: (0, 0)),
                  pl.BlockSpec((None, d, tn), lambda l, j: (l, 0, j)),
                  pl.BlockSpec((None, 1, tn), lambda l, j: (l, 0, j))],
        out_specs=pl.BlockSpec((None, r, tn), lambda l, j: (l, 0, j)),
        out_shape=jax.ShapeDtypeStruct((depth, r, d6), F32),
        compiler_params=_cparams(("parallel", "parallel")),
        name="adaln_mod",
    )(c_all, w_ada, b_ada.reshape(depth, 1, d6))


def _prep_kernel(lb_ref, q1_ref, k1_ref, q2_ref, k2_ref, lbo_ref, lam_ref):
    lg = lb_ref[...]
    depth = lg.shape[0]
    mx = jnp.max(lg, axis=0, keepdims=True)
    e = jnp.exp(lg - mx)
    p = e / jnp.sum(e, axis=0, keepdims=True)
    run = p[0:1]
    rows = [jnp.zeros_like(run)]
    for l in range(1, depth):
        run = run + p[l:l + 1]
        rows.append(run - p[0:1])
    lbo_ref[...] = jnp.concatenate(rows, axis=0)
    s1 = jnp.sum(q1_ref[...] * k1_ref[...], axis=1, keepdims=True)
    s2 = jnp.sum(q2_ref[...] * k2_ref[...], axis=1, keepdims=True)
    lam_ref[...] = jnp.broadcast_to(jnp.exp(s1) - jnp.exp(s2), lam_ref.shape)


def _prep_call(lb_logits, lq1, lk1, lq2, lk2):
    depth, n = lb_logits.shape
    return pl.pallas_call(
        _prep_kernel,
        out_shape=(jax.ShapeDtypeStruct((depth, n), F32),
                   jax.ShapeDtypeStruct((depth, LANES), F32)),
        name="layer_scalars",
    )(lb_logits, lq1, lk1, lq2, lk2)


_IN_SEGS = (
    ("qsb", W_SB, BF16, DH_SB ** -0.5), ("ksb", W_SB, F32, None), ("vsb", W_SB, F32, None),
    ("qhg", NK_HG, F32, None), ("ahg", NK_HG, F32, None), ("ihg", W_HG, BF16, None), ("ghg", W_HG, F32, None),
    ("qdf", NQK_DF, BF16, DH_DF ** -0.5 * math.log2(math.e)), ("kdf", NQK_DF, F32, None), ("vdf", W_DF, F32, None),
)
_BF16_COPIES = ("ksb", "vsb", "kdf", "vdf")
_T_COPIES = ("qdf", "vdf")
_IN_NAMES = [s[0] for s in _IN_SEGS] + [n + "_b" for n in _BF16_COPIES] + [n + "_t" for n in _T_COPIES]


def _inproj_kernel(x_ref, g_ref, sc_ref, sh_ref, w_ref, *out_refs):
    x = x_ref[...]
    y = x * lax.rsqrt(jnp.mean(x * x, axis=-1, keepdims=True) + EPS) * g_ref[...]
    h = (y * (1.0 + sc_ref[...]) + sh_ref[...]).astype(BF16)
    outs = dict(zip(_IN_NAMES, out_refs))
    off = 0
    for name, width, dt, scale in _IN_SEGS:
        z = _mm(h, w_ref[:, off:off + width])
        off += width
        if scale is not None:
            z = z * scale
        outs[name][...] = z.astype(dt)
        if name in _BF16_COPIES:
            outs[name + "_b"][...] = z.astype(BF16)
        if name in _T_COPIES:
            outs[name + "_t"][...] = z.T.astype(BF16)


def _inproj_call(x, g, mod, w_in_b, tm):
    G, T, D = x.shape
    R = mod.shape[1]
    rb = 1 if R == 1 else tm
    nt = T // tm
    width = w_in_b.shape[1]
    mrow = (lambda comp: (lambda b, i: (b, 0 if R == 1 else i, comp)))
    names = _IN_NAMES
    widths = {s[0]: s[1] for s in _IN_SEGS}
    dts = {s[0]: s[2] for s in _IN_SEGS}
    for n in _BF16_COPIES:
        widths[n + "_b"], dts[n + "_b"] = widths[n], BF16

    def out_spec(n):
        if n.endswith("_t"):
            return pl.BlockSpec((None, widths[n[:-2]], tm), lambda b, i: (b, 0, i))
        return pl.BlockSpec((None, tm, widths[n]), lambda b, i: (b, i, 0))

    def out_shape(n):
        if n.endswith("_t"):
            return jax.ShapeDtypeStruct((G, widths[n[:-2]], T), BF16)
        return jax.ShapeDtypeStruct((G, T, widths[n]), dts[n])

    outs = pl.pallas_call(
        _inproj_kernel,
        grid=(G, nt),
        in_specs=[pl.BlockSpec((None, tm, D), lambda b, i: (b, i, 0)),
                  pl.BlockSpec((1, D), lambda b, i: (0, 0)),
                  pl.BlockSpec((None, rb, D), mrow(1)),
                  pl.BlockSpec((None, rb, D), mrow(0)),
                  pl.BlockSpec((D, width), lambda b, i: (0, 0))],
        out_specs=[out_spec(n) for n in names],
        out_shape=[out_shape(n) for n in names],
        compiler_params=_cparams(("parallel", "parallel")),
        name="norm_inproj",
    )(x, g.reshape(1, D), mod, mod, w_in_b)
    return dict(zip(names, outs))


def _sb_block(q, k, v, c, acc, u, mask, transposed=False):
    z = _mm(q, k) if transposed else _mm_nt(q, k)
    ls_pos, ls_neg = _log_sigmoid_pair(z)
    if mask is not None:
        ls_neg = jnp.where(mask, ls_neg, 0.0)
    surv = _mm_f32_lhs(ls_neg, u) + c
    w = jnp.exp(ls_pos + surv)
    if mask is not None:
        w = jnp.where(mask, w, 0.0)
    w = w.astype(BF16)
    acc = acc + (_mm_nt(w, v) if transposed else _mm(w, v))
    c = c + jnp.sum(ls_neg, axis=1, keepdims=True)
    return c, acc


def _sb_prompt_kernel(q_ref, k_ref, v_ref, u_ref, o_ref, *, tq):
    qi = pl.program_id(1)
    u = u_ref[...]
    row = lax.broadcasted_iota(jnp.int32, (tq, tq), 0)
    col = lax.broadcasted_iota(jnp.int32, (tq, tq), 1)
    diag_mask = col < row
    heads = [slice(h * DH_SB, (h + 1) * DH_SB) for h in range(H_SB)]
    qs = [q_ref[:, hs] for hs in heads]

    def visit(kb, state, mask):
        st = pl.multiple_of(kb * tq, tq)
        return tuple(_sb_block(q, k_ref[pl.ds(st, tq), hs], v_ref[pl.ds(st, tq), hs], c, acc, u, mask)
                     for q, hs, (c, acc) in zip(qs, heads, state))

    init = tuple((jnp.zeros((tq, 1), F32), jnp.zeros((tq, DH_SB), F32)) for _ in heads)
    state = visit(qi, init, diag_mask)

    def cond(carry):
        kb, state = carry
        alive = functools.reduce(jnp.maximum, [c for c, _ in state])
        return jnp.logical_and(kb >= 0, jnp.max(alive) > SB_DEAD)

    def body(carry):
        kb, state = carry
        return kb - 1, visit(kb, state, None)

    _, state = lax.while_loop(cond, body, (qi - 1, state))
    for hs, (_, acc) in zip(heads, state):
        o_ref[:, hs] = acc


def _suffix_matrix(n):
    return jnp.asarray(np.tril(np.ones((n, n), np.float32), -1), BF16)


def _sb_prompt_call(q, k, v, tq):
    B, T, W = q.shape
    return pl.pallas_call(
        functools.partial(_sb_prompt_kernel, tq=tq),
        grid=(B, T // tq),
        in_specs=[pl.BlockSpec((None, tq, W), lambda b, i: (b, i, 0)),
                  pl.BlockSpec((None, T, W), lambda b, i: (b, 0, 0)),
                  pl.BlockSpec((None, T, W), lambda b, i: (b, 0, 0)),
                  pl.BlockSpec((tq, tq), lambda b, i: (0, 0))],
        out_specs=pl.BlockSpec((None, tq, W), lambda b, i: (b, i, 0)),
        out_shape=jax.ShapeDtypeStruct((B, T, W), F32),
        compiler_params=_cparams(("parallel", "parallel")),
        name="sb_prompt",
    )(q, k, v, _suffix_matrix(tq))


def _softmax_step(s, v, m, l, acc):
    m_new = jnp.maximum(m, jnp.max(s, axis=1, keepdims=True))
    alpha = jnp.exp2(m - m_new)
    p = jnp.exp2(s - m_new)
    l = alpha * l + jnp.sum(p, axis=1, keepdims=True)
    acc = alpha * acc + _mm(p.astype(BF16), v)
    return m_new, l, acc


_DF_GROUP = 4


def _df_prompt_kernel(lam_ref, qt_ref, k_ref, vt_ref, o_ref, *, tq):
    qi = pl.program_id(2)
    qt = qt_ref[...]
    first = lax.broadcasted_iota(jnp.int32, qt.shape, 0) < DH_DF
    zero = jnp.zeros((), BF16)
    qbd = jnp.concatenate([jnp.where(first, qt, zero), jnp.where(first, zero, qt)], axis=1)

    def scores(kb):
        return _mm(k_ref[pl.ds(pl.multiple_of(kb * tq, tq), tq), :], qbd)

    def update(kb, s, carry):
        m, l, acc = carry
        m_new = jnp.maximum(m, jnp.max(s, axis=0, keepdims=True))
        alpha = jnp.exp2(m - m_new)
        p = jnp.exp2(s - m_new)
        l = alpha * l + jnp.sum(p, axis=0, keepdims=True)
        acc = alpha * acc + _mm(vt_ref[:, pl.ds(pl.multiple_of(kb * tq, tq), tq)], p.astype(BF16))
        return m_new, l, acc

    key = lax.broadcasted_iota(jnp.int32, (tq, 2 * tq), 0)
    qry = lax.broadcasted_iota(jnp.int32, (tq, 2 * tq), 1) % tq
    init = (jnp.full((1, 2 * tq), -jnp.inf, F32), jnp.zeros((1, 2 * tq), F32), jnp.zeros((DV_DF, 2 * tq), F32))
    carry = update(qi, jnp.where(key <= qry, scores(qi), -jnp.inf), init)

    def group(j, carry):
        blocks = [_DF_GROUP * j + i for i in range(_DF_GROUP)]
        for kb, s in zip(blocks, [scores(kb) for kb in blocks]):
            carry = update(kb, s, carry)
        return carry

    n_groups = qi // _DF_GROUP
    carry = lax.fori_loop(0, n_groups, group, carry)
    _, l, acc = lax.fori_loop(n_groups * _DF_GROUP, qi, lambda kb, c: update(kb, scores(kb), c), carry)
    o = acc / l
    o_ref[...] = (o[:, :tq] - lam_ref[0, 0] * o[:, tq:]).T


def _df_prompt_call(lam, q_t, k, v_t, tq):
    B, T, _ = k.shape
    blk = 2 * DH_DF
    return pl.pallas_call(
        functools.partial(_df_prompt_kernel, tq=tq),
        grid=(B, H_DF, T // tq),
        in_specs=[pl.BlockSpec(memory_space=pltpu.SMEM),
                  pl.BlockSpec((None, blk, tq), lambda b, h, i: (b, h, i)),
                  pl.BlockSpec((None, T, blk), lambda b, h, i: (b, 0, h)),
                  pl.BlockSpec((None, DV_DF, T), lambda b, h, i: (b, h, 0))],
        out_specs=pl.BlockSpec((None, tq, DV_DF), lambda b, h, i: (b, i, h)),
        out_shape=jax.ShapeDtypeStruct((B, T, W_DF), F32),
        compiler_params=_cparams(("parallel", "parallel", "parallel")),
        name="df_prompt",
    )(lam, q_t, k, v_t)


def _hgrn_consts(C):
    levels = int(math.log2(C))
    t = np.arange(C)
    tril = (t[:, None] >= t[None, :]).astype(np.float32)
    lmat, rmat, masks = [], [], []
    for lv in range(levels):
        w = C >> (lv + 1)
        blk = t // (2 * w)
        second = (t % (2 * w)) >= w
        mid = blk * 2 * w + w - 1
        j = t[None, :]
        lmat.append(((j > mid[:, None]) & (j <= t[:, None]) & second[:, None]).astype(np.float32))
        rmat.append(((j > t[:, None]) & (j <= mid[:, None]) & (~second)[:, None]).astype(np.float32))
        masks.append(((blk[:, None] == blk[None, :]) & second[:, None] & (~second)[None, :]).astype(np.float32))
    masks.append(np.eye(C, dtype=np.float32))
    lr = np.concatenate([tril] + lmat + rmat, axis=0)
    return jnp.asarray(lr, BF16), jnp.asarray(np.stack(masks), F32), levels


def _hgrn_kernel(q_ref, a_ref, i_ref, lb_ref, s0_ref, lr_ref, mk_ref, o_ref, sT_ref, st_scr, *, C, levels):
    ci = pl.program_id(1)

    @pl.when(ci == 0)
    def _():
        st_scr[...] = s0_ref[...]

    a = a_ref[...]
    lb = lb_ref[...]
    lsa, _ = _log_sigmoid_pair(a)
    x0 = jnp.log(lb)
    y0 = jnp.log1p(-lb) + lsa
    log_f = jnp.maximum(x0, y0) + jnp.log1p(jnp.exp(-jnp.abs(x0 - y0)))
    kk = (1.0 - lb) / (1.0 + jnp.exp(a))
    sums = _mm_f32_rhs(lr_ref[...], log_f)
    b = sums[:C]
    q = q_ref[...]
    q_in = (q * jnp.exp(b)).astype(BF16)
    k_out = (kk * jnp.exp(b[C - 1:C] - b)).astype(BF16)
    qb = q.astype(BF16)
    kb = kk.astype(BF16)
    dec_l = jnp.exp(b[C - 1:C])
    for h in range(H_HG):
        ks = slice(h * DK_HG, (h + 1) * DK_HG)
        vs = slice(h * DV_HG, (h + 1) * DV_HG)
        att = mk_ref[levels] * _mm_nt(qb[:, ks], kb[:, ks])
        for lv in range(levels):
            dq = sums[(1 + lv) * C:(2 + lv) * C, ks]
            dk = sums[(1 + levels + lv) * C:(2 + levels + lv) * C, ks]
            ql = (q[:, ks] * jnp.exp(dq)).astype(BF16)
            kl = (kk[:, ks] * jnp.exp(dk)).astype(BF16)
            att = att + mk_ref[lv] * _mm_nt(ql, kl)
        iv = i_ref[:, vs]
        sT = st_scr[h]
        o_ref[:, vs] = _mm_nt(q_in[:, ks], sT.astype(BF16)) + _mm(att.astype(BF16), iv)
        st_scr[h] = sT * dec_l[:, ks] + _mm_tn(iv, k_out[:, ks])

    @pl.when(ci == pl.num_programs(1) - 1)
    def _():
        sT_ref[...] = st_scr[...]


def _hgrn_call(q, a, iv, lb_row, s0T, C):
    B, T, _ = q.shape
    lr, masks, levels = _hgrn_consts(C)
    nlr = lr.shape[0]
    return pl.pallas_call(
        functools.partial(_hgrn_kernel, C=C, levels=levels),
        grid=(B, T // C),
        in_specs=[pl.BlockSpec((None, C, NK_HG), lambda b, i: (b, i, 0)),
                  pl.BlockSpec((None, C, NK_HG), lambda b, i: (b, i, 0)),
                  pl.BlockSpec((None, C, W_HG), lambda b, i: (b, i, 0)),
                  pl.BlockSpec((1, NK_HG), lambda b, i: (0, 0)),
                  pl.BlockSpec((None, H_HG, DV_HG, DK_HG), lambda b, i: (b, 0, 0, 0)),
                  pl.BlockSpec((nlr, C), lambda b, i: (0, 0)),
                  pl.BlockSpec((levels + 1, C, C), lambda b, i: (0, 0, 0))],
        out_specs=[pl.BlockSpec((None, C, W_HG), lambda b, i: (b, i, 0)),
                   pl.BlockSpec((None, H_HG, DV_HG, DK_HG), lambda b, i: (b, 0, 0, 0))],
        out_shape=[jax.ShapeDtypeStruct((B, T, W_HG), F32),
                   jax.ShapeDtypeStruct((B, H_HG, DV_HG, DK_HG), F32)],
        scratch_shapes=[pltpu.VMEM((H_HG, DV_HG, DK_HG), F32)],
        compiler_params=_cparams(("parallel", "arbitrary")),
        name="hgrn2",
    )(q, a, iv, lb_row, s0T, lr, masks)


def _page_copies(cache_k, cache_v, kbuf, vbuf, semk, semv, layer, pages, slot):
    cps = []
    for j, page in enumerate(pages):
        cps.append(pltpu.make_async_copy(cache_k.at[layer, page], kbuf.at[slot, j], semk.at[slot]))
        cps.append(pltpu.make_async_copy(cache_v.at[layer, page], vbuf.at[slot, j], semv.at[slot]))
    return cps


def _sb_sample_kernel(pt_ref, q_ref, kn_ref, vn_ref, u_ref, un_ref, ck_ref, cv_ref, o_ref,
                      kbuf, vbuf, semk, semv, *, layer, P, n_tok):
    b = pl.program_id(0)
    nb = pl.num_programs(0)
    n_pages = pt_ref.shape[1]
    nch = n_pages // P
    slot = b % 2

    def copies(bb, ch, sl):
        pages = [pt_ref[bb, n_pages - 1 - ch * P - j] for j in range(P)]
        return _page_copies(ck_ref, cv_ref, kbuf, vbuf, semk, semv, layer, pages, sl)

    @pl.when(b == 0)
    def _():
        for cp in copies(0, 0, 0):
            cp.start()

    @pl.when(b + 1 < nb)
    def _():
        for cp in copies(b + 1, 0, 1 - slot):
            cp.start()

    q = q_ref[...]
    rows = q.shape[0]
    nn = kn_ref.shape[0]
    t_of_row = lax.broadcasted_iota(jnp.int32, (rows, nn), 0) % n_tok
    j_new = lax.broadcasted_iota(jnp.int32, (rows, nn), 1)
    mask_new = jnp.logical_and(j_new < t_of_row, j_new < n_tok)
    c, acc = _sb_block(q, kn_ref[...], vn_ref[...], jnp.zeros((rows, 1), F32),
                       jnp.zeros((rows, W_SB), F32), un_ref[...], mask_new)
    u = u_ref[...]

    def chunk(sl, c, acc):
        for j in range(P):
            kt = kbuf[sl, j].reshape(W_SB, PAGE).astype(BF16)
            vt = vbuf[sl, j].reshape(W_SB, PAGE).astype(BF16)
            c, acc = _sb_block(q, kt, vt, c, acc, u, None, transposed=True)
        return c, acc

    for cp in copies(b, 0, slot):
        cp.wait()
    c, acc = chunk(slot, c, acc)

    def cond(carry):
        ch, c, _ = carry
        return jnp.logical_and(ch < nch, jnp.max(c) > SB_DEAD)

    def body(carry):
        ch, c, acc = carry
        cps = copies(b, ch, 2)
        for cp in cps:
            cp.start()
        for cp in cps:
            cp.wait()
        c, acc = chunk(2, c, acc)
        return ch + 1, c, acc

    _, _, acc = lax.while_loop(cond, body, (jnp.int32(1), c, acc))
    for h in range(H_SB):
        hs = slice(h * DH_SB, (h + 1) * DH_SB)
        o_ref[:, hs] = acc[h * n_tok:(h + 1) * n_tok, hs]


def _block_diag_queries(q, groups):
    B, T, W = q.shape
    d = W // groups
    gid_col = (jnp.arange(W) // d)[None, None, None, :]
    gid_row = jnp.arange(groups)[None, :, None, None]
    out = jnp.where(gid_col == gid_row, q[:, None, :, :], jnp.zeros((), q.dtype))
    return out.reshape(B, groups * T, W)


def _pad_rows(x, n):
    return jnp.pad(x, ((0, 0), (0, n - x.shape[1]), (0, 0)))


_NEW_PAD = 16


def _sb_sample_call(page_table, q, k_new, v_new, cache_k, cache_v, layer, P=4):
    B, n_tok, W = q.shape
    page_shape = cache_k.shape[2:]
    qbd = _block_diag_queries(q, H_SB)
    rows = qbd.shape[1]
    kn, vn = _pad_rows(k_new, _NEW_PAD), _pad_rows(v_new, _NEW_PAD)
    grid_spec = pltpu.PrefetchScalarGridSpec(
        num_scalar_prefetch=1,
        grid=(B,),
        in_specs=[pl.BlockSpec((None, rows, W), lambda b, pt: (b, 0, 0)),
                  pl.BlockSpec((None, _NEW_PAD, W), lambda b, pt: (b, 0, 0)),
                  pl.BlockSpec((None, _NEW_PAD, W), lambda b, pt: (b, 0, 0)),
                  pl.BlockSpec((PAGE, PAGE), lambda b, pt: (0, 0)),
                  pl.BlockSpec((_NEW_PAD, _NEW_PAD), lambda b, pt: (0, 0)),
                  pl.BlockSpec(memory_space=pl.ANY),
                  pl.BlockSpec(memory_space=pl.ANY)],
        out_specs=pl.BlockSpec((None, n_tok, W), lambda b, pt: (b, 0, 0)),
        scratch_shapes=[pltpu.VMEM((3, P) + page_shape, F32), pltpu.VMEM((3, P) + page_shape, F32),
                        pltpu.SemaphoreType.DMA((3,)), pltpu.SemaphoreType.DMA((3,))])
    return pl.pallas_call(
        functools.partial(_sb_sample_kernel, layer=layer, P=P, n_tok=n_tok),
        grid_spec=grid_spec,
        out_shape=jax.ShapeDtypeStruct((B, n_tok, W), F32),
        compiler_params=_cparams(("arbitrary",)),
        name="sb_sample",
    )(page_table, qbd, kn, vn, _suffix_matrix(PAGE), _suffix_matrix(_NEW_PAD), cache_k, cache_v)


def _df_sample_kernel(pt_ref, lam_ref, q_ref, kn_ref, vn_ref, ck_ref, cv_ref, o_ref,
                      kbuf, vbuf, semk, semv, m_scr, l_scr, acc_scr, *, layer, P, n_tok):
    g = pl.program_id(0)
    total = pl.num_programs(0)
    n_pages = pt_ref.shape[1]
    nch = n_pages // P
    c = g % nch
    slot = g % 2

    page_rows = PAGE * H_DF

    def copies(gg, sl):
        bb = gg // nch
        cc = gg % nch
        cps = []
        for j in range(P):
            page = pt_ref[bb, cc * P + j]
            dst = pl.ds(j * page_rows, page_rows)
            cps.append(pltpu.make_async_copy(ck_ref.at[layer, page], kbuf.at[sl, dst], semk.at[sl]))
            cps.append(pltpu.make_async_copy(cv_ref.at[layer, page], vbuf.at[sl, dst], semv.at[sl]))
        return cps

    @pl.when(g == 0)
    def _():
        for cp in copies(0, 0):
            cp.start()

    @pl.when(g + 1 < total)
    def _():
        for cp in copies(g + 1, 1 - slot):
            cp.start()

    rows = q_ref.shape[1]

    @pl.when(c == 0)
    def _():
        nn = kn_ref.shape[0]
        t_of_row = lax.broadcasted_iota(jnp.int32, (rows, nn), 0) % n_tok
        j_new = lax.broadcasted_iota(jnp.int32, (rows, nn), 1)
        for h in range(H_DF):
            hs = slice(h * DV_DF, (h + 1) * DV_DF)
            s = jnp.where(j_new <= t_of_row, _mm_nt(q_ref[h], kn_ref[:, hs]), -jnp.inf)
            m = jnp.max(s, axis=1, keepdims=True)
            p = jnp.exp2(s - m)
            m_scr[h] = m
            l_scr[h] = jnp.sum(p, axis=1, keepdims=True)
            acc_scr[h] = _mm(p.astype(BF16), vn_ref[:, hs])

    for cp in copies(g, slot):
        cp.wait()
    for h in range(H_DF):
        head_rows = pl.ds(h, P * PAGE, stride=H_DF)
        k = kbuf[slot, head_rows, :].astype(BF16)
        v = vbuf[slot, head_rows, :].astype(BF16)
        m, l, acc = _softmax_step(_mm_nt(q_ref[h], k), v, m_scr[h], l_scr[h], acc_scr[h])
        m_scr[h] = m
        l_scr[h] = l
        acc_scr[h] = acc

    @pl.when(c == nch - 1)
    def _():
        for h in range(H_DF):
            o = acc_scr[h] / l_scr[h]
            o_ref[:, h * DV_DF:(h + 1) * DV_DF] = o[:n_tok] - lam_ref[0, 0] * o[n_tok:]


def _df_sample_call(page_table, lam, q, k_new, v_new, cache_k, cache_v, layer, P=16):
    B, n_tok, W = q.shape
    qh = q.reshape(B, n_tok, H_DF, 2 * DH_DF).transpose(0, 2, 1, 3)
    map_of_col = (jnp.arange(2 * DH_DF) // DH_DF)[None, None, None, None, :]
    map_of_row = jnp.arange(2)[None, None, :, None, None]
    qbd = jnp.where(map_of_col == map_of_row, qh[:, :, None], jnp.zeros((), q.dtype))
    rows = 2 * n_tok
    qbd = qbd.reshape(B, H_DF, rows, 2 * DH_DF)
    kn, vn = _pad_rows(k_new, _NEW_PAD), _pad_rows(v_new, _NEW_PAD)
    n_pages = page_table.shape[1]
    nch = n_pages // P
    page_rows, lanes = cache_k.shape[2:]
    grid_spec = pltpu.PrefetchScalarGridSpec(
        num_scalar_prefetch=1,
        grid=(B * nch,),
        in_specs=[pl.BlockSpec(memory_space=pltpu.SMEM),
                  pl.BlockSpec((None, H_DF, rows, 2 * DH_DF), lambda g, pt: (g // nch, 0, 0, 0)),
                  pl.BlockSpec((None, _NEW_PAD, W), lambda g, pt: (g // nch, 0, 0)),
                  pl.BlockSpec((None, _NEW_PAD, W_DF), lambda g, pt: (g // nch, 0, 0)),
                  pl.BlockSpec(memory_space=pl.ANY),
                  pl.BlockSpec(memory_space=pl.ANY)],
        out_specs=pl.BlockSpec((None, n_tok, W_DF), lambda g, pt: (g // nch, 0, 0)),
        scratch_shapes=[pltpu.VMEM((2, P * page_rows, lanes), F32), pltpu.VMEM((2, P * page_rows, lanes), F32),
                        pltpu.SemaphoreType.DMA((2,)), pltpu.SemaphoreType.DMA((2,)),
                        pltpu.VMEM((H_DF, rows, 1), F32), pltpu.VMEM((H_DF, rows, 1), F32),
                        pltpu.VMEM((H_DF, rows, DV_DF), F32)])
    return pl.pallas_call(
        functools.partial(_df_sample_kernel, layer=layer, P=P, n_tok=n_tok),
        grid_spec=grid_spec,
        out_shape=jax.ShapeDtypeStruct((B, n_tok, W_DF), F32),
        compiler_params=_cparams(("arbitrary",)),
        name="df_sample",
    )(page_table, lam, qbd, kn, vn, cache_k, cache_v)


def _group_mean_matrix(width, d):
    g = np.arange(width) // d
    return jnp.asarray((g[:, None] == g[None, :]).astype(np.float32) / d, BF16)


def _head_norm(o, group_mean, gain):
    return o * lax.rsqrt(_mm_f32_lhs(o * o, group_mean) + EPS) * gain


def _rms_mod(x, g, sc, sh):
    y = x * lax.rsqrt(jnp.mean(x * x, axis=-1, keepdims=True) + EPS) * g
    return y * (1.0 + sc) + sh


def _merge_kernel(x_ref, osb_ref, ohg_ref, ghg_ref, odf_ref, gsb_ref, gnh_ref, gdf_ref, m64_ref, m128_ref,
                  w_ref, g1_ref, n2_ref, sc2_ref, sh2_ref, x_out, h2_out, *, df_scale):
    m64 = m64_ref[...]
    gate = ghg_ref[...]
    o_sb = _head_norm(osb_ref[...], m64, gsb_ref[...]).astype(BF16)
    o_hg = (_head_norm(ohg_ref[...], m64, gnh_ref[...]) * (gate / (1.0 + jnp.exp(-gate)))).astype(BF16)
    o_df = (_head_norm(odf_ref[...], m128_ref[...], gdf_ref[...]) * df_scale).astype(BF16)
    proj = (_mm(o_sb, w_ref[0:W_SB, :]) + _mm(o_hg, w_ref[W_SB:W_SB + W_HG, :])
            + _mm(o_df, w_ref[W_SB + W_HG:, :]))
    x = x_ref[...] + g1_ref[...] * proj
    x_out[...] = x
    h2_out[...] = _rms_mod(x, n2_ref[...], sc2_ref[...], sh2_ref[...]).T.astype(BF16)


def _merge_call(x, o_sb, o_hg, gate, o_df, gn_sb, gn_hg, gn_df, w_out_b, mod, norm2_g, df_scale, tm):
    G, T, D = x.shape
    R = mod.shape[1]
    rb = 1 if R == 1 else tm
    mrow = (lambda comp: (lambda b, i: (b, 0 if R == 1 else i, comp)))
    tok = lambda w: pl.BlockSpec((None, tm, w), lambda b, i: (b, i, 0))
    full = lambda a: pl.BlockSpec(a.shape, lambda b, i: (0,) * a.ndim)
    m64, m128 = _group_mean_matrix(W_SB, DH_SB), _group_mean_matrix(W_DF, DV_DF)
    gsb, ghg, gdf, n2 = gn_sb.reshape(1, -1), gn_hg.reshape(1, -1), gn_df.reshape(1, -1), norm2_g.reshape(1, D)
    return pl.pallas_call(
        functools.partial(_merge_kernel, df_scale=df_scale),
        grid=(G, T // tm),
        in_specs=[tok(D), tok(W_SB), tok(W_HG), tok(W_HG), tok(W_DF), full(gsb), full(ghg), full(gdf),
                  full(m64), full(m128), full(w_out_b),
                  pl.BlockSpec((None, rb, D), mrow(2)), full(n2),
                  pl.BlockSpec((None, rb, D), mrow(4)), pl.BlockSpec((None, rb, D), mrow(3))],
        out_specs=[tok(D), pl.BlockSpec((D, tm), lambda b, i: (0, b * (T // tm) + i))],
        out_shape=[jax.ShapeDtypeStruct((G, T, D), F32), jax.ShapeDtypeStruct((D, G * T), BF16)],
        compiler_params=_cparams(("parallel", "parallel")),
        name="merge_outproj",
    )(x, o_sb, o_hg, gate, o_df, gsb, ghg, gdf, m64, m128, w_out_b, mod, n2, mod, mod)


def _final_norm_kernel(x_ref, g_ref, o_ref):
    x = x_ref[...]
    o_ref[...] = x * lax.rsqrt(jnp.mean(x * x, axis=-1, keepdims=True) + EPS) * g_ref[...]


def _final_norm_call(x, g, tm):
    G, T, D = x.shape
    return pl.pallas_call(
        _final_norm_kernel,
        grid=(G, T // tm),
        in_specs=[pl.BlockSpec((None, tm, D), lambda b, i: (b, i, 0)), pl.BlockSpec((1, D), lambda b, i: (0, 0))],
        out_specs=pl.BlockSpec((None, tm, D), lambda b, i: (b, i, 0)),
        out_shape=jax.ShapeDtypeStruct((G, T, D), F32),
        compiler_params=_cparams(("parallel", "parallel")),
        name="final_norm",
    )(x, g.reshape(1, D))


def _pick_max(x, iota, break_ties):
    m = jnp.max(x, axis=0, keepdims=True)
    sel = x == m
    if break_ties:
        sel = iota == jnp.min(jnp.where(sel, iota, x.shape[0]), axis=0, keepdims=True)
    return m, sel


def _top16(arrays, break_ties):
    n, L = arrays[0].shape
    iota = lax.broadcasted_iota(jnp.int32, (n, L), 0)
    row16 = lax.broadcasted_iota(jnp.int32, (PEER_TOPK, L), 0)

    def one(j, s, rank, vals):
        m, sel = _pick_max(s, iota, break_ties)
        rank = jnp.where(sel, lax.convert_element_type(j, F32), rank)
        return jnp.where(sel, -jnp.inf, s), rank, jnp.where(row16 == j, m, vals)

    def body(j, carry):
        return tuple(one(j, *c) for c in carry)

    init = tuple((s, jnp.full((n, L), float(PEER_TOPK), F32), jnp.zeros((PEER_TOPK, L), F32)) for s in arrays)
    out = lax.fori_loop(0, PEER_TOPK, body, init)
    return [(rank, vals) for _, rank, vals in out]


_CAND_ROWS = (16,) + (8,) * 7


def _peer_route_kernel(h2t_ref, wq_ref, k1_ref, k2_ref, rank2_ref, e2_ref, cnt_ref, e1_ref, qt_scr):
    qt_scr[...] = _mm(wq_ref[...], h2t_ref[...])
    L = LANES
    dk = k1_ref.shape[2]
    ncand = sum(_CAND_ROWS) + 8
    iota_c = lax.broadcasted_iota(jnp.int32, (ncand, L), 0)
    row8 = lax.broadcasted_iota(jnp.int32, (8, L), 0)

    def route(h, lanes):
        base = pl.multiple_of(h * 2 * dk, 2 * dk)
        s1 = _mm(k1_ref[h], qt_scr[pl.ds(base, dk), lanes].astype(BF16))
        s2 = _mm(k2_ref[h], qt_scr[pl.ds(base + dk, dk), lanes].astype(BF16))

        def select(break_ties):
            (rank1, v1), (rank2, v2) = _top16((s1, s2), break_ties)
            cand = jnp.concatenate(
                [v1[r:r + 1] + v2[0:n] for r, n in enumerate(_CAND_ROWS)] + [v1[8:16] + v2[0:1]], axis=0)
            m0 = v1[0:1] + v2[0:1]

            def pick(j, carry):
                cand, chosen, z = carry
                m, sel = _pick_max(cand, iota_c, break_ties)
                return jnp.where(sel, -jnp.inf, cand), jnp.where(sel, 1.0, chosen), z + jnp.exp(m - m0)

            _, chosen, z = lax.fori_loop(0, PEER_TOPK, pick,
                                         (cand, jnp.zeros((ncand, L), F32), jnp.zeros((1, L), F32)))
            return rank1, rank2, v1[0:1], v2[0:1], chosen, z

        fast = select(False)
        picked = [jnp.sum(jnp.where(r < float(PEER_TOPK), 1.0, 0.0), axis=0, keepdims=True) for r in fast[:2]]
        picked.append(jnp.sum(fast[4], axis=0, keepdims=True))
        excess = functools.reduce(jnp.maximum, picked)
        rank1, rank2, top1, top2, chosen, z = lax.cond(
            jnp.max(excess) > float(PEER_TOPK), lambda: select(True), lambda: fast)
        low = jnp.zeros((8, L), F32)
        off = 0
        for r, n in enumerate(_CAND_ROWS):
            low = jnp.where(row8 == r, jnp.sum(chosen[off:off + n], axis=0, keepdims=True), low)
            off += n
        cnt16 = jnp.concatenate([low, chosen[off:off + 8]], axis=0)
        cnt = jnp.zeros_like(s1)
        for r in range(PEER_TOPK):
            cnt = jnp.where(rank1 == float(r), cnt16[r:r + 1], cnt)
        rank2_ref[h, :, lanes] = rank2
        e2_ref[h, :, lanes] = jnp.exp(s2 - top2)
        cnt_ref[h, :, lanes] = cnt
        e1_ref[h, :, lanes] = jnp.exp(s1 - top1) / z

    def head(h, _):
        for t0 in range(0, qt_scr.shape[1], L):
            route(h, slice(t0, t0 + L))
        return 0

    lax.fori_loop(0, k1_ref.shape[0], head, 0)


def _peer_route_call(h2t, wq_t, k1, k2):
    D, N = h2t.shape
    heads, keys, dk = k1.shape
    tn = 2 * LANES if N % (2 * LANES) == 0 else LANES
    tbl = pl.BlockSpec((heads, keys, tn), lambda i: (0, 0, i))
    shp = jax.ShapeDtypeStruct((heads, keys, N), F32)
    return pl.pallas_call(
        _peer_route_kernel,
        grid=(N // tn,),
        in_specs=[pl.BlockSpec((D, tn), lambda i: (0, i)),
                  pl.BlockSpec(wq_t.shape, lambda i: (0, 0)),
                  pl.BlockSpec(k1.shape, lambda i: (0, 0, 0)),
                  pl.BlockSpec(k2.shape, lambda i: (0, 0, 0))],
        out_specs=[tbl, tbl, tbl, tbl],
        out_shape=[shp, shp, shp, shp],
        scratch_shapes=[pltpu.VMEM((wq_t.shape[0], tn), F32)],
        compiler_params=_cparams(("parallel",)),
        name="peer_route",
    )(h2t, wq_t, k1, k2)


def _peer_dense_kernel(x_ref, g2_ref, h2t_ref, u_ref, vt_ref, rank2_ref, e2_ref, cnt_ref, e1_ref, o_ref,
                       acc_scr, rank2_scr, e2_scr, *, te):
    et = pl.program_id(2)
    heads, keys, _ = rank2_ref.shape

    @pl.when(et == 0)
    def _():
        acc_scr[...] = jnp.zeros_like(acc_scr)
        rank2_scr[...] = rank2_ref[...].astype(BF16)
        e2_scr[...] = e2_ref[...].astype(BF16)

    tn = h2t_ref.shape[1]
    n_i1 = te // keys
    zero = jnp.zeros((), BF16)
    sub = 16

    def pre_activation(rows, lanes):
        return _mm(u_ref[rows, :], h2t_ref[:, lanes])

    def routed(j0, n_j, lanes, a):
        act = (a * (1.0 + lax.erf(a * (2.0 ** -0.5)))).astype(BF16)
        width = act.shape[1]

        def row_bf16(ref, h, i1):
            return jnp.broadcast_to(ref[h, pl.ds(i1, 1), lanes], (sub, width)).astype(BF16)

        pieces = []
        for j in range(n_j):
            i1 = et * n_i1 + j0 + j
            cnt_rows = [row_bf16(cnt_ref, h, i1) for h in range(heads)]
            e1_rows = [row_bf16(e1_ref, h, i1) for h in range(heads)]
            for r in range(keys // sub):
                rs = slice(r * sub, (r + 1) * sub)
                g = jnp.zeros((sub, width), BF16)
                for h in range(heads):
                    g = g + jnp.where(rank2_scr[h, rs, lanes] < cnt_rows[h], e1_rows[h] * e2_scr[h, rs, lanes], zero)
                pieces.append(g * act[j * keys + r * sub:j * keys + (r + 1) * sub])
        return jnp.concatenate(pieces, axis=0)

    group = 2 * LANES if tn % (2 * LANES) == 0 else tn
    n_chunks = 4 if n_i1 % 4 == 0 else 1
    cj = n_i1 // n_chunks
    blocks = [(c, slice(t0, t0 + group)) for t0 in range(0, tn, group) for c in range(n_chunks)]
    pre = [pre_activation(slice(c * cj * keys, (c + 1) * cj * keys), lanes) for c, lanes in blocks]
    for (c, lanes), a in zip(blocks, pre):
        cols = slice(c * cj * keys, (c + 1) * cj * keys)
        acc_scr[:, lanes] += _mm(vt_ref[:, cols], routed(c * cj, cj, lanes, a))

    @pl.when(et == pl.num_programs(2) - 1)
    def _():
        o_ref[...] = x_ref[...] + g2_ref[...] * acc_scr[...].T


def _peer_dense_call(x, mod, h2t, u_b, vt_b, tables, tn, te):
    G, T, D = x.shape
    E = u_b.shape[0]
    R = mod.shape[1]
    rb = 1 if R == 1 else tn
    nt = T // tn
    heads, keys, _ = tables[0].shape
    tbl = pl.BlockSpec((heads, keys, tn), lambda b, i, e: (0, 0, b * nt + i))
    tok = pl.BlockSpec((None, tn, D), lambda b, i, e: (b, i, 0))
    return pl.pallas_call(
        functools.partial(_peer_dense_kernel, te=te),
        grid=(G, nt, E // te),
        in_specs=[tok, pl.BlockSpec((None, rb, D), lambda b, i, e: (b, 0 if R == 1 else i, 5)),
                  pl.BlockSpec((D, tn), lambda b, i, e: (0, b * nt + i)),
                  pl.BlockSpec((te, D), lambda b, i, e: (e, 0)),
                  pl.BlockSpec((D, te), lambda b, i, e: (0, e)),
                  tbl, tbl, tbl, tbl],
        out_specs=tok,
        out_shape=jax.ShapeDtypeStruct((G, T, D), F32),
        scratch_shapes=[pltpu.VMEM((D, tn), F32),
                        pltpu.VMEM((heads, keys, tn), BF16), pltpu.VMEM((heads, keys, tn), BF16)],
        compiler_params=_cparams(("parallel", "parallel", "arbitrary")),
        name="peer_dense",
    )(x, mod, h2t, u_b, vt_b, *tables)


def _layer(l, x, mod, page_table, caches, s0T, lb_row, lam_total, lam_init, lw, sizes):
    tm, tq_sb, tq_df, chunk, tn_peer, te_peer = sizes
    G, T, D = x.shape
    z = _inproj_call(x, lw["norm1_g"], mod, lw["w_in"], tm)
    if caches is None:
        o_sb = _sb_prompt_call(z["qsb"], z["ksb_b"], z["vsb_b"], tq_sb)
        o_df = _df_prompt_call(lam_total, z["qdf_t"], z["kdf_b"], z["vdf_t"], tq_df)
        tok = lambda a: a
    else:
        B = page_table.shape[0]
        per = lambda a: a.reshape(B, T // B, a.shape[-1])
        tok = lambda a: a.reshape(1, T, a.shape[-1])
        ck_sb, cv_sb, ck_df, cv_df = caches
        o_sb = tok(_sb_sample_call(page_table, per(z["qsb"]), per(z["ksb_b"]), per(z["vsb_b"]), ck_sb, cv_sb, l))
        o_df = tok(_df_sample_call(page_table, lam_total, per(z["qdf"]), per(z["kdf_b"]), per(z["vdf_b"]),
                                   ck_df, cv_df, l))
        z = dict(z, qhg=per(z["qhg"]), ahg=per(z["ahg"]), ihg=per(z["ihg"]))
    o_hg, sT = _hgrn_call(z["qhg"], z["ahg"], z["ihg"], lb_row, s0T, chunk)
    x, h2t = _merge_call(x, o_sb, tok(o_hg), z["ghg"], o_df, lw["gn_sb"], lw["gn_hg"], lw["gn_df"], lw["w_out"],
                         mod, lw["norm2_g"], 1.0 - lam_init, tm)
    tables = _peer_route_call(h2t, lw["w_pq_t"], lw["peer_k1"], lw["peer_k2"])
    x = _peer_dense_call(x, mod, h2t, lw["peer_u"], lw["peer_v_t"], tables, tn_peer, te_peer)
    return x, (z["ksb"], z["vsb"], z["kdf"], z["vdf"], sT)


def kernel(x_prompt, x_sample, cache_sb_k, cache_sb_v, cache_df_k, cache_df_v, state_hgrn, page_table,
           c_prompt, c_sample, w_ada, b_ada, norm1_g, norm2_g, w_in, lb_logits, lam_q1, lam_k1, lam_q2, lam_k2,
           gn_sb, gn_hg, gn_df, w_out, w_pq, peer_k1, peer_k2, peer_u, peer_v, final_g):
    depth = w_ada.shape[0]
    Bp, Tp, D = x_prompt.shape
    Bs, Ts, _ = x_sample.shape
    n_pool = cache_sb_k.shape[1]

    c_all = jnp.concatenate([c_prompt, c_sample], axis=0)
    pad = (-c_all.shape[0]) % 8
    mod_all = _mod_call(jnp.pad(c_all, ((0, pad), (0, 0))), w_ada, b_ada)
    lb_all, lam_diff = _prep_call(lb_logits, lam_q1, lam_k1, lam_q2, lam_k2)

    caches = (jnp.transpose(cache_sb_k, (0, 1, 3, 4, 2)), jnp.transpose(cache_sb_v, (0, 1, 3, 4, 2)),
              cache_df_k.reshape(depth, n_pool, PAGE * H_DF, 2 * DH_DF),
              cache_df_v.reshape(depth, n_pool, PAGE * H_DF, DV_DF))
    s0T_sample = jnp.swapaxes(state_hgrn, 3, 4)
    s0T_prompt = jnp.zeros((Bp, H_HG, DV_HG, DK_HG), F32)

    xp = x_prompt
    xs = x_sample.reshape(1, Bs * Ts, D)
    sizes_p = (512, 256, 512, 128, 512, 1024)
    sizes_s = (Bs * Ts, None, None, Ts, Bs * Ts, 1024)
    outs_p, outs_s = [], []
    for l in range(depth):
        lam_init = 0.8 - 0.6 * math.exp(-0.3 * l)
        lw = dict(norm1_g=norm1_g[l], norm2_g=norm2_g[l], w_in=w_in[l].astype(BF16), gn_sb=gn_sb[l], gn_hg=gn_hg[l],
                  gn_df=gn_df[l], w_out=w_out[l].astype(BF16), w_pq_t=w_pq[l].T.astype(BF16),
                  peer_k1=peer_k1[l].astype(BF16), peer_k2=peer_k2[l].astype(BF16),
                  peer_u=peer_u[l].astype(BF16), peer_v_t=(0.5 * peer_v[l]).T.astype(BF16))
        lam_total = (lam_diff[l, 0] + lam_init).reshape(1, 1)
        lb_row = lb_all[l].reshape(1, NK_HG)
        mod_p = mod_all[l, :Bp].reshape(Bp, 1, 6 * D)
        mod_s = jnp.repeat(mod_all[l, Bp:Bp + Bs], Ts, axis=0).reshape(1, Bs * Ts, 6 * D)
        xs, os_ = _layer(l, xs, mod_s, page_table, caches, s0T_sample[l], lb_row, lam_total, lam_init, lw, sizes_s)
        xp, op = _layer(l, xp, mod_p, None, None, s0T_prompt, lb_row, lam_total, lam_init, lw, sizes_p)
        outs_p.append(op)
        outs_s.append(os_)

    y_prompt = _final_norm_call(xp, final_g, 512)
    y_sample = _final_norm_call(xs, final_g, Bs * Ts).reshape(Bs, Ts, D)

    def stack(outs, B, T):
        ksb = jnp.stack([o[0] for o in outs]).reshape(depth, B, T, H_SB, DH_SB)
        vsb = jnp.stack([o[1] for o in outs]).reshape(depth, B, T, H_SB, DH_SB)
        kdf = jnp.stack([o[2] for o in outs]).reshape(depth, B, T, H_DF, 2 * DH_DF)
        vdf = jnp.stack([o[3] for o in outs]).reshape(depth, B, T, H_DF, DV_DF)
        st = jnp.swapaxes(jnp.stack([o[4] for o in outs]), 3, 4)
        return ksb, vsb, kdf, vdf, st

    return (y_prompt, y_sample) + stack(outs_p, Bp, Tp) + stack(outs_s, Bs, Ts)
```
